```python
import math
import jax, jax.numpy as jnp
from jax import lax
import numpy as np

D_MODEL = 1024
BATCH = 4
SEQ = 4096
DEPTH = 1

HEAD_DIM = 64
D_MIX = D_MODEL
D_SB = D_MIX // 2
D_MB = D_MIX - D_SB
N_HEADS_SB = D_SB // HEAD_DIM
N_HEADS_MB = D_MB // HEAD_DIM
Q_BLOCK = 128
MOBA_BLOCK = 256
MOBA_TOPK = 3
N_BUCKETS = 32
MAX_DISTANCE = 128
D_FF = -(-8 * D_MODEL // (3 * 256)) * 256
LN_EPS = 1e-5
RMS_EPS = 1e-6
DN_ALPHA = (2 * DEPTH) ** 0.25
DN_BETA = (8 * DEPTH) ** -0.25
NEG_INF = -jnp.inf

kernel_name = 'hybrid_stickbreak_moba_deepnorm_adaln'


def layer_norm(x, g, b):
    xf = x.astype(jnp.float32)
    mu = jnp.mean(xf, axis=-1, keepdims=True)
    var = jnp.mean(jnp.square(xf - mu), axis=-1, keepdims=True)
    return ((xf - mu) * lax.rsqrt(var + LN_EPS) * g + b).astype(x.dtype)


def split_heads(t, n_heads):
    b, s, _ = t.shape
    return t.reshape(b, s, n_heads, HEAD_DIM).transpose(0, 2, 1, 3)


def head_rms_merge(o, g):
    o = o * lax.rsqrt(jnp.mean(jnp.square(o), axis=-1, keepdims=True) + RMS_EPS)
    b, h, s, d = o.shape
    return o.transpose(0, 2, 1, 3).reshape(b, s, h * d) * g


def t5_bucket(dist):
    n = jnp.maximum(dist, 0)
    max_exact = N_BUCKETS // 2
    nf = jnp.maximum(n, 1).astype(jnp.float32)
    large = max_exact + (jnp.log(nf / max_exact) / math.log(MAX_DISTANCE / max_exact)
                         * (N_BUCKETS - max_exact)).astype(jnp.int32)
    large = jnp.minimum(large, N_BUCKETS - 1)
    return jnp.where(n < max_exact, n, large)


def stick_breaking_attention(q, k, v):
    s_len = q.shape[2]
    scale = HEAD_DIM ** -0.5
    outs = []
    for i in range(s_len // Q_BLOCK):
        q0 = i * Q_BLOCK
        kl = q0 + Q_BLOCK
        qb = q[:, :, q0:kl].astype(jnp.float32)
        kp = k[:, :, :kl].astype(jnp.float32)
        vp = v[:, :, :kl].astype(jnp.float32)
        z = jnp.einsum('bhtd,bhsd->bhts', qb, kp) * scale
        t_pos = q0 + jnp.arange(Q_BLOCK)[:, None]
        s_pos = jnp.arange(kl)[None, :]
        past = s_pos < t_pos
        log_1m = jnp.where(past, jax.nn.log_sigmoid(-z), 0.0)
        log_stick = lax.cumsum(log_1m, axis=3, reverse=True) - log_1m
        w = jnp.where(past, jnp.exp(jax.nn.log_sigmoid(z) + log_stick), 0.0)
        outs.append(jnp.einsum('bhts,bhsd->bhtd', w, vp))
    return jnp.concatenate(outs, axis=2)


def moba_attention(q, k, v, rel_bias):
    b, h, s_len, d = q.shape
    scale = HEAD_DIM ** -0.5
    nb = -(-s_len // MOBA_BLOCK)
    pad = nb * MOBA_BLOCK - s_len
    kp = jnp.pad(k, ((0, 0), (0, 0), (0, pad), (0, 0)))
    vp = jnp.pad(v, ((0, 0), (0, 0), (0, pad), (0, 0)))
    kb = kp.reshape(b, h, nb, MOBA_BLOCK, d)
    vb = vp.reshape(b, h, nb, MOBA_BLOCK, d)
    k_mean = jnp.mean(kb.astype(jnp.float32), axis=3)
    pos = jnp.arange(s_len)
    q_blk = pos // MOBA_BLOCK
    gate = jnp.einsum('bhsd,bhnd->bhsn', q.astype(jnp.float32), k_mean)
    fully_past = jnp.arange(nb)[None, :] < q_blk[:, None]
    gate = jnp.where(fully_past, gate, NEG_INF)
    n_sel = min(MOBA_TOPK, nb)
    _, sel = lax.top_k(gate, n_sel)
    sel_valid = sel < q_blk[:, None]
    rel_h = rel_bias.T.astype(jnp.float32)
    head_ids = jnp.arange(h)[None, :, None, None]
    offs = jnp.arange(MOBA_BLOCK)
    gather_blocks = jax.vmap(jax.vmap(lambda blocks, ids: blocks[ids]))

    def query_block(i):
        q0 = i * Q_BLOCK
        qc = lax.dynamic_slice_in_dim(q, q0, Q_BLOCK, axis=2).astype(jnp.float32)
        sc = lax.dynamic_slice_in_dim(sel, q0, Q_BLOCK, axis=2)
        vc = lax.dynamic_slice_in_dim(sel_valid, q0, Q_BLOCK, axis=2)
        t_pos = q0 + jnp.arange(Q_BLOCK)
        logits = []
        for r in range(n_sel):
            idx = sc[..., r]
            k_sel = gather_blocks(kb, idx).astype(jnp.float32)
            k_pos = idx[..., None] * MOBA_BLOCK + offs
            bias = rel_h[head_ids, t5_bucket(t_pos[:, None] - k_pos)]
            lr = jnp.einsum('bhtd,bhtld->bhtl', qc, k_sel) * scale + bias
            logits.append(jnp.where(vc[..., r, None], lr, NEG_INF))
        own_start = (q0 // MOBA_BLOCK) * MOBA_BLOCK
        k_own = lax.dynamic_slice_in_dim(kp, own_start, MOBA_BLOCK, axis=2).astype(jnp.float32)
        v_own = lax.dynamic_slice_in_dim(vp, own_start, MOBA_BLOCK, axis=2).astype(jnp.float32)
        dist = t_pos[:, None] - (own_start + offs)[None, :]
        lo = jnp.einsum('bhtd,bhld->bhtl', qc, k_own) * scale + rel_h[:, t5_bucket(dist)]
        logits.append(jnp.where(dist >= 0, lo, NEG_INF))
        p = jax.nn.softmax(jnp.concatenate(logits, axis=-1), axis=-1)
        out = jnp.einsum('bhtl,bhld->bhtd', p[..., n_sel * MOBA_BLOCK:], v_own)
        for r in range(n_sel):
            v_sel = gather_blocks(vb, sc[..., r]).astype(jnp.float32)
            out = out + jnp.einsum('bhtl,bhtld->bhtd',
                                   p[..., r * MOBA_BLOCK:(r + 1) * MOBA_BLOCK], v_sel)
        return out

    outs = lax.map(query_block, jnp.arange(s_len // Q_BLOCK))
    return outs.transpose(1, 2, 0, 3, 4).reshape(b, h, s_len, d)


def hybrid_layer(x, mod, w_in, sb_gain, mb_gain, w_out, ln1_g, ln1_b,
                 w_gate, w_up, w_down, ln2_g, ln2_b, rel_bias):
    shift_a, scale_a, gate_a, shift_f, scale_f, gate_f = jnp.split(mod[:, None, :], 6, axis=-1)
    u = x * (1 + scale_a) + shift_a
    proj = u @ w_in
    q_sb, k_sb, v_sb, q_mb, k_mb, v_mb = jnp.split(
        proj, [D_SB, 2 * D_SB, 3 * D_SB, 3 * D_SB + D_MB, 3 * D_SB + 2 * D_MB], axis=-1)
    o_sb = stick_breaking_attention(split_heads(q_sb, N_HEADS_SB), split_heads(k_sb, N_HEADS_SB),
                                    split_heads(v_sb, N_HEADS_SB))
    o_mb = moba_attention(split_heads(q_mb, N_HEADS_MB), split_heads(k_mb, N_HEADS_MB),
                          split_heads(v_mb, N_HEADS_MB), rel_bias)
    o = jnp.concatenate([head_rms_merge(o_sb, sb_gain), head_rms_merge(o_mb, mb_gain)],
                        axis=-1).astype(x.dtype)
    x = layer_norm(DN_ALPHA * x + gate_a * (o @ w_out), ln1_g, ln1_b)
    u = x * (1 + scale_f) + shift_f
    f = (jax.nn.silu(u @ w_gate) * (u @ w_up)) @ w_down
    return layer_norm(DN_ALPHA * x + gate_f * f, ln2_g, ln2_b)


def setup_inputs(seed: int = 0) -> dict:
    key = jax.random.key(seed)
    ks = jax.random.split(key, 16)
    f32 = jnp.float32
    n = lambda k, shp: jax.random.normal(k, shp, f32)
    return {
        'x': n(ks[0], (BATCH, SEQ, D_MODEL)),
        'c': n(ks[1], (BATCH, D_MODEL)),
        'w_ada': n(ks[2], (DEPTH, D_MODEL, 6 * D_MODEL)) * D_MODEL ** -0.5,
        'b_ada': 0.02 * n(ks[3], (DEPTH, 6 * D_MODEL)),
        'w_in': n(ks[4], (DEPTH, D_MODEL, 3 * D_MIX)) * D_MODEL ** -0.5,
        'sb_gain': 1.0 + 0.02 * n(ks[5], (DEPTH, D_SB)),
        'mb_gain': 1.0 + 0.02 * n(ks[6], (DEPTH, D_MB)),
        'w_out': n(ks[7], (DEPTH, D_MIX, D_MODEL)) * (D_MIX ** -0.5 * DN_BETA),
        'ln1_g': 1.0 + 0.02 * n(ks[8], (DEPTH, D_MODEL)),
        'ln1_b': 0.02 * n(ks[9], (DEPTH, D_MODEL)),
        'w_gate': n(ks[10], (DEPTH, D_MODEL, D_FF)) * D_MODEL ** -0.5,
        'w_up': n(ks[11], (DEPTH, D_MODEL, D_FF)) * D_MODEL ** -0.5,
        'w_down': n(ks[12], (DEPTH, D_FF, D_MODEL)) * (D_FF ** -0.5 * DN_BETA),
        'ln2_g': 1.0 + 0.02 * n(ks[13], (DEPTH, D_MODEL)),
        'ln2_b': 0.02 * n(ks[14], (DEPTH, D_MODEL)),
        'rel_bias': 0.5 * n(ks[15], (N_BUCKETS, N_HEADS_MB)),
    }


def reference(x, c, w_ada, b_ada, w_in, sb_gain, mb_gain, w_out, ln1_g, ln1_b,
              w_gate, w_up, w_down, ln2_g, ln2_b, rel_bias):
    cond = jax.nn.silu(c)
    for l in range(DEPTH):
        mod = cond @ w_ada[l] + b_ada[l]
        x = hybrid_layer(x, mod, w_in[l], sb_gain[l], mb_gain[l], w_out[l], ln1_g[l], ln1_b[l],
                         w_gate[l], w_up[l], w_down[l], ln2_g[l], ln2_b[l], rel_bias)
    return x
```

```python
import functools
import math

import jax
import jax.numpy as jnp
from jax import lax
from jax.experimental import pallas as pl
from jax.experimental.pallas import tpu as pltpu

HEAD_DIM = 64
MOBA_BLOCK = 256
MOBA_TOPK = 3
N_BUCKETS = 32
MAX_DISTANCE = 128
LN_EPS = 1e-5
RMS_EPS = 1e-6
NEG_BIG = -1e30
LANES = 128
ATT_BLOCK = 256
VMEM_LIMIT = 56 * 1024 * 1024

F32 = jnp.float32
BF16 = jnp.bfloat16


def _cparams(sem):
    return pltpu.CompilerParams(dimension_semantics=sem, vmem_limit_bytes=VMEM_LIMIT)


def _split(a):
    hi = a.astype(BF16)
    lo = (a - hi.astype(F32)).astype(BF16)
    return hi, lo


def _dot(a, b):
    return jnp.dot(a, b, preferred_element_type=F32)


def _dot_nt(a, b):
    return lax.dot_general(a, b, (((1,), (1,)), ((), ())), preferred_element_type=F32)


def _layer_norm(r, g, b):
    mu = jnp.mean(r, axis=-1, keepdims=True)
    d = r - mu
    var = jnp.mean(d * d, axis=-1, keepdims=True)
    return d * lax.rsqrt(var + LN_EPS) * g + b


def _mod_kernel(c_ref, w_ref, b_ref, o_ref):
    c = c_ref[...]
    cond = c * jax.nn.sigmoid(c)
    c_hi, c_lo = _split(cond)
    w_hi, w_lo = _split(w_ref[...])
    o_ref[...] = _dot(c_hi, w_hi) + _dot(c_hi, w_lo) + _dot(c_lo, w_hi) + b_ref[...]


def _modulation(c_pad, w, b):
    rows, d = c_pad.shape
    n = w.shape[1]
    tn = n // 4 if n % (4 * LANES) == 0 else n
    return pl.pallas_call(
        _mod_kernel,
        grid=(n // tn,),
        in_specs=[pl.BlockSpec((rows, d), lambda j: (0, 0)),
                  pl.BlockSpec((d, tn), lambda j: (0, j)),
                  pl.BlockSpec((1, tn), lambda j: (0, j))],
        out_specs=pl.BlockSpec((rows, tn), lambda j: (0, j)),
        out_shape=jax.ShapeDtypeStruct((rows, n), F32),
        compiler_params=_cparams(("arbitrary",)),
        name="adaln_mod",
    )(c_pad, w, b)


def _inproj_kernel(x_ref, mod_ref, wlo_ref, whh_ref, whl_ref, olo_ref, ohi_ref):
    x = x_ref[0]
    shift = mod_ref[0, 0:1, :]
    scale = mod_ref[0, 1:2, :]
    u = x * (1.0 + scale) + shift
    u_hi, u_lo = _split(u)
    olo_ref[0] = _dot(u_hi, wlo_ref[...]).astype(BF16)
    whh = whh_ref[...]
    ohi_ref[0] = _dot(u_hi, whh) + _dot(u_hi, whl_ref[...]) + _dot(u_lo, whh)


def _in_projection(x, mod, w_lo, w_hh, w_hl, tm):
    b, s, d = x.shape
    n_lo, n_hi = w_lo.shape[1], w_hh.shape[1]
    const = lambda bi, i: (0, 0)
    return pl.pallas_call(
        _inproj_kernel,
        grid=(b, s // tm),
        in_specs=[pl.BlockSpec((1, tm, d), lambda bi, i: (bi, i, 0)),
                  pl.BlockSpec((1, 6, d), lambda bi, i: (bi, 0, 0)),
                  pl.BlockSpec((d, n_lo), const),
                  pl.BlockSpec((d, n_hi), const),
                  pl.BlockSpec((d, n_hi), const)],
        out_specs=[pl.BlockSpec((1, tm, n_lo), lambda bi, i: (bi, i, 0)),
                   pl.BlockSpec((1, tm, n_hi), lambda bi, i: (bi, i, 0))],
        out_shape=[jax.ShapeDtypeStruct((b, s, n_lo), BF16),
                   jax.ShapeDtypeStruct((b, s, n_hi), F32)],
        compiler_params=_cparams(("arbitrary", "arbitrary")),
        name="in_proj",
    )(x, mod, w_lo, w_hh, w_hl)


def _head_rms_pair(o0, o1, gain, lane):
    lo_half = lane < HEAD_DIM
    o = jnp.where(lo_half, o0, o1)
    sq = o * o
    ms0 = jnp.sum(jnp.where(lo_half, sq, 0.0), axis=1, keepdims=True)
    ms1 = jnp.sum(jnp.where(lo_half, 0.0, sq), axis=1, keepdims=True)
    ms = jnp.where(lo_half, ms0, ms1) * (1.0 / HEAD_DIM)
    return o * lax.rsqrt(ms + RMS_EPS) * gain


def _sb_kernel(q_ref, k_ref, v_ref, g_ref, o_ref, acc_ref, carry_ref):
    t = ATT_BLOCK
    qi = pl.program_id(2)
    q2 = q_ref[0].astype(F32)
    lane = lax.broadcasted_iota(jnp.int32, (t, LANES), 1)
    row = lax.broadcasted_iota(jnp.int32, (t, t), 0)
    col = lax.broadcasted_iota(jnp.int32, (t, t), 1)
    past = col < row
    tri = jnp.where(row >= col, 1.0, 0.0).astype(BF16)
    scale = HEAD_DIM ** -0.5

    for h in range(2):
        in_head = (lane < HEAD_DIM) if h == 0 else (lane >= HEAD_DIM)
        qh = (jnp.where(in_head, q2, 0.0) * scale).astype(BF16)
        acc_ref[h] = jnp.zeros((t, LANES), F32)
        carry_ref[...] = jnp.zeros((t, LANES), F32)

        def block(kb, diag, qh=qh, h=h):
            start = pl.multiple_of(kb * t, t)
            k2 = k_ref[0, pl.ds(start, t), :]
            v2 = v_ref[0, pl.ds(start, t), :]
            z = _dot_nt(qh, k2)
            lg = -(jnp.maximum(z, 0.0) + jnp.log1p(jnp.exp(-jnp.abs(z))))
            if diag:
                lg = jnp.where(past, lg, 0.0)
            l_hi, l_lo = _split(lg)
            linc = _dot(l_hi, tri) + _dot(l_lo, tri)
            carry = carry_ref[...]
            w = jnp.exp(z + linc + pltpu.repeat(carry, t // LANES, axis=1))
            if diag:
                w = jnp.where(past, w, 0.0)
            acc_ref[h] += _dot(w.astype(BF16), v2)
            carry_ref[...] = carry + jnp.sum(lg, axis=1, keepdims=True)

        block(qi, True)

        def body(i, c):
            block(qi - 1 - i, False)
            return c

        lax.fori_loop(0, qi, body, 0)

    o_ref[0] = _head_rms_pair(acc_ref[0], acc_ref[1], g_ref[...], lane).astype(o_ref.dtype)


def _sb_attention(qkv, gain, n_pairs, q_blk0, k_blk0, v_blk0):
    b, s, _ = qkv.shape
    t = ATT_BLOCK
    return pl.pallas_call(
        _sb_kernel,
        grid=(b, n_pairs, s // t),
        in_specs=[pl.BlockSpec((1, t, LANES), lambda bi, p, i: (bi, i, q_blk0 + p)),
                  pl.BlockSpec((1, s, LANES), lambda bi, p, i: (bi, 0, k_blk0 + p)),
                  pl.BlockSpec((1, s, LANES), lambda bi, p, i: (bi, 0, v_blk0 + p)),
                  pl.BlockSpec((1, LANES), lambda bi, p, i: (0, p))],
        out_specs=pl.BlockSpec((1, t, LANES), lambda bi, p, i: (bi, i, p)),
        out_shape=jax.ShapeDtypeStruct((b, s, n_pairs * LANES), BF16),
        scratch_shapes=[pltpu.VMEM((2, t, LANES), F32), pltpu.VMEM((t, LANES), F32)],
        compiler_params=_cparams(("arbitrary", "arbitrary", "arbitrary")),
        name="sb_attention",
    )(qkv, qkv, qkv, gain)


def _moba_kernel(q_ref, k_ref, v_ref, bias_ref, cb_ref, g_ref, o_ref,
                 kb_ref, km_ref, m_ref, l_ref, acc_ref, sel_ref):
    t = ATT_BLOCK
    rep = t // LANES
    p_id = pl.program_id(1)
    qi = pl.program_id(2)
    nb = k_ref.shape[1] // t
    lane = lax.broadcasted_iota(jnp.int32, (t, LANES), 1)
    lane_f = lane.astype(F32)
    row = lax.broadcasted_iota(jnp.int32, (t, t), 0)
    col = lax.broadcasted_iota(jnp.int32, (t, t), 1)
    causal = col <= row
    scale = HEAD_DIM ** -0.5

    @pl.when(qi == 0)
    def _():
        kf = k_ref[0]
        kb_ref[...] = kf.astype(BF16)
        km_ref[...] = jnp.zeros(km_ref.shape, F32)
        km_ref[0:nb, :] = jnp.mean(kf.reshape(nb, t, LANES), axis=1)

    q2 = q_ref[0]
    km_hi, km_lo = _split(km_ref[...])

    def update(h, s, v2):
        m_prev = m_ref[...]
        m_new = jnp.maximum(m_prev, jnp.max(s, axis=1, keepdims=True))
        alpha = jnp.exp(m_prev - m_new)
        p = jnp.exp(s - pltpu.repeat(m_new, rep, axis=1))
        l_ref[...] = alpha * l_ref[...] + jnp.sum(p, axis=1, keepdims=True)
        acc_ref[h] = alpha * acc_ref[h] + _dot(p.astype(BF16), v2)
        m_ref[...] = m_new

    for h in range(2):
        in_head = (lane < HEAD_DIM) if h == 0 else (lane >= HEAD_DIM)
        qh = jnp.where(in_head, q2, 0.0)
        qs = (qh * scale).astype(BF16)

        q_hi, q_lo = _split(qh)
        gate = _dot_nt(q_hi, km_hi) + _dot_nt(q_hi, km_lo) + _dot_nt(q_lo, km_hi)
        valid = lane < qi
        g = jnp.where(valid, gate, -jnp.inf)
        mask = jnp.full((t, LANES), NEG_BIG, F32)
        for _ in range(MOBA_TOPK):
            mx = jnp.max(g, axis=1, keepdims=True)
            first = jnp.min(jnp.where(g == mx, lane_f, float(LANES)), axis=1, keepdims=True)
            pick = lane_f == first
            mask = jnp.where(pick, 0.0, mask)
            g = jnp.where(pick, -jnp.inf, g)
        sel_ref[...] = jnp.where(valid, mask, NEG_BIG)

        def sel_col(n):
            return jnp.sum(jnp.where(lane == n, sel_ref[...], 0.0), axis=1, keepdims=True)

        start = pl.multiple_of(qi * t, t)
        s_own = _dot_nt(qs, kb_ref[pl.ds(start, t), :]) + bias_ref[h, 0]
        s_own = jnp.where(causal, s_own, NEG_BIG)
        m_ref[...] = jnp.full((t, LANES), NEG_BIG, F32)
        l_ref[...] = jnp.zeros((t, LANES), F32)
        acc_ref[h] = jnp.zeros((t, LANES), F32)
        update(h, s_own, v_ref[0, pl.ds(start, t), :])

        @pl.when(qi > 0)
        def _(h=h, qs=qs, sel_col=sel_col, update=update):
            n = qi - 1
            st = pl.multiple_of(n * t, t)
            s_prev = _dot_nt(qs, kb_ref[pl.ds(st, t), :]) + bias_ref[h, 1] + sel_col(n)
            update(h, s_prev, v_ref[0, pl.ds(st, t), :])

        cb = cb_ref[pl.ds(2 * p_id + h, 1), :]
        cb2 = pltpu.repeat(cb, rep, axis=1)

        def body(n, c, h=h, qs=qs, sel_col=sel_col, update=update, cb2=cb2):
            st = pl.multiple_of(n * t, t)
            s_far = _dot_nt(qs, kb_ref[pl.ds(st, t), :]) + cb2 + sel_col(n)
            update(h, s_far, v_ref[0, pl.ds(st, t), :])
            return c

        lax.fori_loop(0, jnp.maximum(qi - 1, 0), body, 0)
        acc_ref[h] = acc_ref[h] / l_ref[...]

    o_ref[0] = _head_rms_pair(acc_ref[0], acc_ref[1], g_ref[...], lane).astype(o_ref.dtype)


def _moba_attention(qk, qkv, bias, cbias, gain, n_pairs, q_blk0, k_blk0, v_blk0):
    b, s, _ = qk.shape
    t = ATT_BLOCK
    n_heads = bias.shape[0]
    return pl.pallas_call(
        _moba_kernel,
        grid=(b, n_pairs, s // t),
        in_specs=[pl.BlockSpec((1, t, LANES), lambda bi, p, i: (bi, i, q_blk0 + p)),
                  pl.BlockSpec((1, s, LANES), lambda bi, p, i: (bi, 0, k_blk0 + p)),
                  pl.BlockSpec((1, s, LANES), lambda bi, p, i: (bi, 0, v_blk0 + p)),
                  pl.BlockSpec((2, 2, t, t), lambda bi, p, i: (p, 0, 0, 0)),
                  pl.BlockSpec((n_heads, LANES), lambda bi, p, i: (0, 0)),
                  pl.BlockSpec((1, LANES), lambda bi, p, i: (0, p))],
        out_specs=pl.BlockSpec((1, t, LANES), lambda bi, p, i: (bi, i, p)),
        out_shape=jax.ShapeDtypeStruct((b, s, n_pairs * LANES), BF16),
        scratch_shapes=[pltpu.VMEM((s, LANES), BF16),
                        pltpu.VMEM((LANES, LANES), F32),
                        pltpu.VMEM((t, LANES), F32),
                        pltpu.VMEM((t, LANES), F32),
                        pltpu.VMEM((2, t, LANES), F32),
                        pltpu.VMEM((t, LANES), F32)],
        compiler_params=_cparams(("arbitrary", "arbitrary", "arbitrary")),
        name="moba_attention",
    )(qk, qk, qkv, bias, cbias, gain)


def _outproj_kernel(osb_ref, omb_ref, x_ref, mod_ref, w1_ref, w2_ref, g_ref, b_ref, o_ref, *, alpha):
    y = _dot(osb_ref[0], w1_ref[...]) + _dot(omb_ref[0], w2_ref[...])
    gate = mod_ref[0, 2:3, :]
    r = alpha * x_ref[0] + gate * y
    o_ref[0] = _layer_norm(r, g_ref[...], b_ref[...])


def _out_projection(o_sb, o_mb, x, mod, w1, w2, ln_g, ln_b, alpha, tm):
    b, s, d = x.shape
    h1, h2 = w1.shape[0], w2.shape[0]
    const = lambda bi, i: (0, 0)
    return pl.pallas_call(
        functools.partial(_outproj_kernel, alpha=alpha),
        grid=(b, s // tm),
        in_specs=[pl.BlockSpec((1, tm, h1), lambda bi, i: (bi, i, 0)),
                  pl.BlockSpec((1, tm, h2), lambda bi, i: (bi, i, 0)),
                  pl.BlockSpec((1, tm, d), lambda bi, i: (bi, i, 0)),
                  pl.BlockSpec((1, 6, d), lambda bi, i: (bi, 0, 0)),
                  pl.BlockSpec((h1, d), const),
                  pl.BlockSpec((h2, d), const),
                  pl.BlockSpec((1, d), const),
                  pl.BlockSpec((1, d), const)],
        out_specs=pl.BlockSpec((1, tm, d), lambda bi, i: (bi, i, 0)),
        out_shape=jax.ShapeDtypeStruct((b, s, d), F32),
        compiler_params=_cparams(("arbitrary", "arbitrary")),
        name="out_proj_ln",
    )(o_sb, o_mb, x, mod, w1, w2, ln_g, ln_b)


def _ffn_kernel(x_ref, mod_ref, wg_ref, wu_ref, wd_ref, g_ref, b_ref, o_ref, u_ref, acc_ref, *, alpha):
    j = pl.program_id(2)

    @pl.when(j == 0)
    def _():
        shift = mod_ref[0, 3:4, :]
        scale = mod_ref[0, 4:5, :]
        u_ref[...] = (x_ref[0] * (1.0 + scale) + shift).astype(BF16)
        acc_ref[...] = jnp.zeros(acc_ref.shape, F32)

    u = u_ref[...]
    hg = _dot(u, wg_ref[...])
    hu = _dot(u, wu_ref[...])
    hidden = (hg * jax.nn.sigmoid(hg) * hu).astype(BF16)
    acc_ref[...] += _dot(hidden, wd_ref[...])

    @pl.when(j == pl.num_programs(2) - 1)
    def _():
        gate = mod_ref[0, 5:6, :]
        r = alpha * x_ref[0] + gate * acc_ref[...]
        o_ref[0] = _layer_norm(r, g_ref[...], b_ref[...])


def _ff_tile(d_ff):
    best = None
    for nf in range(1, d_ff // LANES + 1):
        if d_ff % nf == 0 and (d_ff // nf) % LANES == 0 and d_ff // nf <= 1536:
            best = d_ff // nf
            break
    return best if best is not None else d_ff


def _feed_forward(x, mod, wg, wu, wd, ln_g, ln_b, alpha, tm):
    b, s, d = x.shape
    d_ff = wg.shape[1]
    tf = _ff_tile(d_ff)
    return pl.pallas_call(
        functools.partial(_ffn_kernel, alpha=alpha),
        grid=(b, s // tm, d_ff // tf),
        in_specs=[pl.BlockSpec((1, tm, d), lambda bi, i, j: (bi, i, 0)),
                  pl.BlockSpec((1, 6, d), lambda bi, i, j: (bi, 0, 0)),
                  pl.BlockSpec((d, tf), lambda bi, i, j: (0, j)),
                  pl.BlockSpec((d, tf), lambda bi, i, j: (0, j)),
                  pl.BlockSpec((tf, d), lambda bi, i, j: (j, 0)),
                  pl.BlockSpec((1, d), lambda bi, i, j: (0, 0)),
                  pl.BlockSpec((1, d), lambda bi, i, j: (0, 0))],
        out_specs=pl.BlockSpec((1, tm, d), lambda bi, i, j: (bi, i, 0)),
        out_shape=jax.ShapeDtypeStruct((b, s, d), F32),
        scratch_shapes=[pltpu.VMEM((tm, d), BF16), pltpu.VMEM((tm, d), F32)],
        compiler_params=_cparams(("arbitrary", "arbitrary", "arbitrary")),
        name="ffn_ln",
    )(x, mod, wg, wu, wd, ln_g, ln_b)


def _t5_bucket(dist):
    n = jnp.maximum(dist, 0)
    max_exact = N_BUCKETS // 2
    nf = jnp.maximum(n, 1).astype(F32)
    large = max_exact + (jnp.log(nf / max_exact) / math.log(MAX_DISTANCE / max_exact)
                         * (N_BUCKETS - max_exact)).astype(jnp.int32)
    large = jnp.minimum(large, N_BUCKETS - 1)
    return jnp.where(n < max_exact, n, large)


def _moba_bias_tiles(rel_bias):
    t = ATT_BLOCK
    i = jnp.arange(t)[:, None]
    j = jnp.arange(t)[None, :]
    rel_h = rel_bias.T.astype(F32)
    own = rel_h[:, _t5_bucket(i - j)]
    prev = rel_h[:, _t5_bucket(t + i - j)]
    far = rel_h[:, _t5_bucket(jnp.asarray(2 * t))]
    return jnp.stack([own, prev], axis=1), jnp.broadcast_to(far[:, None], (rel_h.shape[0], LANES))


def kernel(x, c, w_ada, b_ada, w_in, sb_gain, mb_gain, w_out, ln1_g, ln1_b,
           w_gate, w_up, w_down, ln2_g, ln2_b, rel_bias):
    depth = w_ada.shape[0]
    b, s, d = x.shape
    d_sb = sb_gain.shape[1]
    d_mb = mb_gain.shape[1]
    assert ATT_BLOCK == MOBA_BLOCK and MOBA_BLOCK >= 2 * MAX_DISTANCE
    assert s % ATT_BLOCK == 0 and s // ATT_BLOCK <= LANES
    assert d_sb % LANES == 0 and d_mb % LANES == 0 and d % LANES == 0
    alpha = float((2 * depth) ** 0.25)
    tm = min(512, s)
    sb_pairs, mb_pairs = d_sb // LANES, d_mb // LANES

    c_pad = jnp.pad(c, ((0, (-c.shape[0]) % 8), (0, 0)))
    bias_tiles, cbias = _moba_bias_tiles(rel_bias)

    for l in range(depth):
        mod = _modulation(c_pad, w_ada[l], b_ada[l][None, :])[:b].reshape(b, 6, d)
        w = w_in[l]
        w_lo = jnp.concatenate([w[:, :3 * d_sb], w[:, 3 * d_sb + 2 * d_mb:]], axis=1).astype(BF16)
        w_hi = w[:, 3 * d_sb:3 * d_sb + 2 * d_mb]
        w_hh = w_hi.astype(BF16)
        w_hl = (w_hi - w_hh.astype(F32)).astype(BF16)
        qkv, qk_mb = _in_projection(x, mod, w_lo, w_hh, w_hl, tm)

        o_sb = _sb_attention(qkv, sb_gain[l][None, :], sb_pairs, 0, sb_pairs, 2 * sb_pairs)
        o_mb = _moba_attention(qk_mb, qkv, bias_tiles, cbias, mb_gain[l][None, :], mb_pairs,
                               0, mb_pairs, 3 * sb_pairs)

        wo = w_out[l].astype(BF16)
        x = _out_projection(o_sb, o_mb, x, mod, wo[:d_sb], wo[d_sb:], ln1_g[l][None, :], ln1_b[l][None, :],
                            alpha, tm)
        x = _feed_forward(x, mod, w_gate[l].astype(BF16), w_up[l].astype(BF16), w_down[l].astype(BF16),
                          ln2_g[l][None, :], ln2_b[l][None, :], alpha, tm)
    return x
```

```python
import functools
import math

import jax
import jax.numpy as jnp
from jax import lax
from jax.experimental import pallas as pl
from jax.experimental.pallas import tpu as pltpu

HEAD_DIM = 64
MOBA_BLOCK = 256
MOBA_TOPK = 3
N_BUCKETS = 32
MAX_DISTANCE = 128
LN_EPS = 1e-5
RMS_EPS = 1e-6
NEG_BIG = -1e30
LANES = 128
ATT_BLOCK = 256
VMEM_LIMIT = 56 * 1024 * 1024

F32 = jnp.float32
BF16 = jnp.bfloat16


def _cparams(sem):
    return pltpu.CompilerParams(dimension_semantics=sem, vmem_limit_bytes=VMEM_LIMIT)


def _split(a):
    hi = a.astype(BF16)
    lo = (a - hi.astype(F32)).astype(BF16)
    return hi, lo


def _dot(a, b):
    return jnp.dot(a, b, preferred_element_type=F32)


def _dot_nt(a, b):
    return lax.dot_general(a, b, (((1,), (1,)), ((), ())), preferred_element_type=F32)


def _layer_norm(r, g, b):
    mu = jnp.mean(r, axis=-1, keepdims=True)
    d = r - mu
    var = jnp.mean(d * d, axis=-1, keepdims=True)
    return d * lax.rsqrt(var + LN_EPS) * g + b


def _mod_kernel(c_ref, w_ref, b_ref, o_ref):
    c = c_ref[...]
    cond = c * jax.nn.sigmoid(c)
    c_hi, c_lo = _split(cond)
    w_hi, w_lo = _split(w_ref[...])
    o_ref[...] = _dot(c_hi, w_hi) + _dot(c_hi, w_lo) + _dot(c_lo, w_hi) + b_ref[...]


def _modulation(c_pad, w, b):
    rows, d = c_pad.shape
    n = w.shape[1]
    tn = n // 4 if n % (4 * LANES) == 0 else n
    return pl.pallas_call(
        _mod_kernel,
        grid=(n // tn,),
        in_specs=[pl.BlockSpec((rows, d), lambda j: (0, 0)),
                  pl.BlockSpec((d, tn), lambda j: (0, j)),
                  pl.BlockSpec((1, tn), lambda j: (0, j))],
        out_specs=pl.BlockSpec((rows, tn), lambda j: (0, j)),
        out_shape=jax.ShapeDtypeStruct((rows, n), F32),
        compiler_params=_cparams(("arbitrary",)),
        name="adaln_mod",
    )(c_pad, w, b)


def _inproj_kernel(x_ref, mod_ref, wlo_ref, whh_ref, whl_ref, olo_ref, ohi_ref):
    x = x_ref[0]
    shift = mod_ref[0, 0:1, :]
    scale = mod_ref[0, 1:2, :]
    u = x * (1.0 + scale) + shift
    u_hi, u_lo = _split(u)
    olo_ref[0] = _dot(u_hi, wlo_ref[...]).astype(BF16)
    whh = whh_ref[...]
    ohi_ref[0] = _dot(u_hi, whh) + _dot(u_hi, whl_ref[...]) + _dot(u_lo, whh)


def _in_projection(x, mod, w_lo, w_hh, w_hl, tm):
    b, s, d = x.shape
    n_lo, n_hi = w_lo.shape[1], w_hh.shape[1]
    const = lambda bi, i: (0, 0)
    return pl.pallas_call(
        _inproj_kernel,
        grid=(b, s // tm),
        in_specs=[pl.BlockSpec((1, tm, d), lambda bi, i: (bi, i, 0)),
                  pl.BlockSpec((1, 6, d), lambda bi, i: (bi, 0, 0)),
                  pl.BlockSpec((d, n_lo), const),
                  pl.BlockSpec((d, n_hi), const),
                  pl.BlockSpec((d, n_hi), const)],
        out_specs=[pl.BlockSpec((1, tm, n_lo), lambda bi, i: (bi, i, 0)),
                   pl.BlockSpec((1, tm, n_hi), lambda bi, i: (bi, i, 0))],
        out_shape=[jax.ShapeDtypeStruct((b, s, n_lo), BF16),
                   jax.ShapeDtypeStruct((b, s, n_hi), F32)],
        compiler_params=_cparams(("arbitrary", "arbitrary")),
        name="in_proj",
    )(x, mod, w_lo, w_hh, w_hl)


def _store_values_transposed(v_ref, vt_ref):
    t = ATT_BLOCK
    for n in range(vt_ref.shape[0]):
        vt_ref[n] = v_ref[0, n * t:(n + 1) * t, :].astype(F32).T.astype(BF16)


def _head_rms_store(o0_t, o1_t, gain, o_ref):
    def norm(o_t):
        ms = jnp.mean(o_t * o_t, axis=0, keepdims=True)
        return o_t * lax.rsqrt(ms + RMS_EPS)
    both = jnp.concatenate([norm(o0_t), norm(o1_t)], axis=0)
    o_ref[0] = (both.T * gain).astype(o_ref.dtype)


def _for_tiles(lo, hi, body, carry):
    cnt = hi - lo
    odd = cnt % 2
    carry = lax.cond(odd == 1, lambda c: body((lo,), c), lambda c: c, carry)
    first = lo + odd
    return lax.fori_loop(0, cnt // 2, lambda i, c: body((first + 2 * i, first + 2 * i + 1), c), carry)


def _masked_heads(q2, scale):
    lane = lax.broadcasted_iota(jnp.int32, q2.shape, 1)
    lo = lane < HEAD_DIM
    return [(jnp.where(lo, q2, 0.0) * scale).astype(BF16),
            (jnp.where(lo, 0.0, q2) * scale).astype(BF16)]


def _sb_kernel(q_ref, k_ref, v_ref, g_ref, o_ref, vt_ref, acc_ref, z_ref, w_ref):
    t = ATT_BLOCK
    qi = pl.program_id(2)

    @pl.when(qi == 0)
    def _():
        _store_values_transposed(v_ref, vt_ref)

    qh = _masked_heads(q_ref[0].astype(F32), HEAD_DIM ** -0.5)
    row = lax.broadcasted_iota(jnp.int32, (t, t), 0)
    col = lax.broadcasted_iota(jnp.int32, (t, t), 1)
    past = row < col
    tri = jnp.where(col >= row, 1.0, 0.0).astype(BF16)
    last_row = lax.broadcasted_iota(jnp.int32, (8, t), 0) == 7

    def logits(tiles, c):
        for n in tiles:
            start = pl.multiple_of(n * t, t)
            k2 = k_ref[0, pl.ds(start, t), :]
            for h in range(2):
                z_ref[h, n] = _dot_nt(k2, qh[h])
        return c

    _for_tiles(0, qi + 1, logits, 0)

    def weights(n, h, carry, diag):
        z = z_ref[h, n]
        neg_abs = pltpu.bitcast(pltpu.bitcast(z, jnp.uint32) | jnp.uint32(0x80000000), F32)
        a = jnp.maximum(z, 0.0) + jnp.log(1.0 + jnp.exp(neg_abs))
        if diag:
            a = jnp.where(past, a, 0.0)
        a = jnp.concatenate([a[:t - 8], a[t - 8:] + jnp.where(last_row, carry, 0.0)], axis=0)
        ainc = _dot(tri, a.astype(BF16))
        w = jnp.exp(z - ainc)
        if diag:
            w = jnp.where(past, w, 0.0)
        w_ref[h, n] = w.astype(BF16)
        return ainc[0:1, :]

    zero = jnp.zeros((1, t), F32)
    carry = tuple(weights(qi, h, zero, True) for h in range(2))

    def past_blocks(steps, c):
        c = list(c)
        for j in steps:
            for h in range(2):
                c[h] = weights(qi - 1 - j, h, c[h], False)
        return tuple(c)

    _for_tiles(0, qi, past_blocks, carry)

    def values(tiles, c):
        for n in tiles:
            for h in range(2):
                acc_ref[h] += _dot(vt_ref[n, h * HEAD_DIM:(h + 1) * HEAD_DIM, :], w_ref[h, n])
        return c

    acc_ref[...] = jnp.zeros(acc_ref.shape, F32)
    _for_tiles(0, qi + 1, values, 0)
    _head_rms_store(acc_ref[0], acc_ref[1], g_ref[...], o_ref)


def _sb_attention(qkv, gain, n_pairs, q_blk0, k_blk0, v_blk0):
    b, s, _ = qkv.shape
    t = ATT_BLOCK
    return pl.pallas_call(
        _sb_kernel,
        grid=(b, n_pairs, s // t),
        in_specs=[pl.BlockSpec((1, t, LANES), lambda bi, p, i: (bi, i, q_blk0 + p)),
                  pl.BlockSpec((1, s, LANES), lambda bi, p, i: (bi, 0, k_blk0 + p)),
                  pl.BlockSpec((1, s, LANES), lambda bi, p, i: (bi, 0, v_blk0 + p)),
                  pl.BlockSpec((1, LANES), lambda bi, p, i: (0, p))],
        out_specs=pl.BlockSpec((1, t, LANES), lambda bi, p, i: (bi, i, p)),
        out_shape=jax.ShapeDtypeStruct((b, s, n_pairs * LANES), BF16),
        scratch_shapes=[pltpu.VMEM((s // t, LANES, t), BF16),
                        pltpu.VMEM((2, HEAD_DIM, t), F32),
                        pltpu.VMEM((2, s // t, t, t), F32),
                        pltpu.VMEM((2, s // t, t, t), BF16)],
        compiler_params=_cparams(("arbitrary", "arbitrary", "arbitrary")),
        name="sb_attention",
    )(qkv, qkv, qkv, gain)


def _moba_kernel(q_ref, k_ref, v_ref, bias_ref, cb_ref, g_ref, o_ref,
                 kb_ref, km_ref, vt_ref, sel_ref, acc_ref, s_ref):
    t = ATT_BLOCK
    p_id = pl.program_id(1)
    qi = pl.program_id(2)
    nb = k_ref.shape[1] // t

    @pl.when(qi == 0)
    def _():
        kf = k_ref[0]
        kb_ref[...] = kf.astype(BF16)
        km_ref[...] = jnp.mean(kf.reshape(nb, t, LANES), axis=1)
        _store_values_transposed(v_ref, vt_ref)

    q2 = q_ref[0]
    lane = lax.broadcasted_iota(jnp.int32, (t, LANES), 1)
    qs = _masked_heads(q2, HEAD_DIM ** -0.5)
    row = lax.broadcasted_iota(jnp.int32, (t, t), 0)
    col = lax.broadcasted_iota(jnp.int32, (t, t), 1)
    causal = row <= col
    blk = lax.broadcasted_iota(jnp.int32, (nb, t), 0)
    blk_f = blk.astype(F32)
    fully_past = blk < qi
    km_hi, km_lo = _split(km_ref[...])

    for h in range(2):
        qh = jnp.where(lane < HEAD_DIM, q2, 0.0) if h == 0 else jnp.where(lane < HEAD_DIM, 0.0, q2)
        q_hi, q_lo = _split(qh)
        gate = _dot_nt(km_hi, q_hi) + _dot_nt(km_lo, q_hi) + _dot_nt(km_hi, q_lo)
        g = jnp.where(fully_past, gate, -jnp.inf)
        mask = jnp.full((nb, t), NEG_BIG, F32)
        for _ in range(MOBA_TOPK):
            mx = jnp.max(g, axis=0, keepdims=True)
            first = jnp.min(jnp.where(g == mx, blk_f, float(nb)), axis=0, keepdims=True)
            pick = blk_f == first
            mask = jnp.where(pick, 0.0, mask)
            g = jnp.where(pick, -jnp.inf, g)
        sel_ref[h] = jnp.where(fully_past, mask, NEG_BIG)

    def scores(n, h):
        start = pl.multiple_of(n * t, t)
        return _dot_nt(kb_ref[pl.ds(start, t), :], qs[h])

    mx = []
    for h in range(2):
        s = jnp.where(causal, scores(qi, h) + bias_ref[h, 0], NEG_BIG)
        s_ref[h, qi] = s
        mx.append(jnp.max(s, axis=0, keepdims=True))

    def prev_block(m):
        n = qi - 1
        out = []
        for h in range(2):
            s = scores(n, h) + bias_ref[h, 1] + sel_ref[h, pl.ds(n, 1), :]
            s_ref[h, n] = s
            out.append(jnp.maximum(m[h], jnp.max(s, axis=0, keepdims=True)))
        return tuple(out)

    mx = lax.cond(qi > 0, prev_block, lambda m: m, tuple(mx))

    def far_blocks(tiles, m):
        ss = [[scores(n, h) for h in range(2)] for n in tiles]
        out = list(m)
        for j, n in enumerate(tiles):
            for h in range(2):
                cb = cb_ref[pl.ds(2 * p_id + h, 1), :]
                s = ss[j][h] + (sel_ref[h, pl.ds(n, 1), :] + cb)
                s_ref[h, n] = s
                out[h] = jnp.maximum(out[h], jnp.max(s, axis=0, keepdims=True))
        return tuple(out)

    mx = _for_tiles(0, jnp.maximum(qi - 1, 0), far_blocks, mx)

    def weigh(tiles, l):
        out = list(l)
        for n in tiles:
            for h in range(2):
                p = jnp.exp(s_ref[h, n] - mx[h])
                out[h] = out[h] + jnp.sum(p, axis=0, keepdims=True)
                acc_ref[h] += _dot(vt_ref[n, h * HEAD_DIM:(h + 1) * HEAD_DIM, :], p.astype(BF16))
        return tuple(out)

    acc_ref[...] = jnp.zeros(acc_ref.shape, F32)
    zero = jnp.zeros((1, t), F32)
    l = _for_tiles(0, qi + 1, weigh, (zero, zero))
    _head_rms_store(acc_ref[0] / l[0], acc_ref[1] / l[1], g_ref[...], o_ref)


def _moba_attention(qk, qkv, bias, cbias, gain, n_pairs, q_blk0, k_blk0, v_blk0):
    b, s, _ = qk.shape
    t = ATT_BLOCK
    nb = s // t
    n_heads = bias.shape[0]
    return pl.pallas_call(
        _moba_kernel,
        grid=(b, n_pairs, nb),
        in_specs=[pl.BlockSpec((1, t, LANES), lambda bi, p, i: (bi, i, q_blk0 + p)),
                  pl.BlockSpec((1, s, LANES), lambda bi, p, i: (bi, 0, k_blk0 + p)),
                  pl.BlockSpec((1, s, LANES), lambda bi, p, i: (bi, 0, v_blk0 + p)),
                  pl.BlockSpec((2, 2, t, t), lambda bi, p, i: (p, 0, 0, 0)),
                  pl.BlockSpec((n_heads, t), lambda bi, p, i: (0, 0)),
                  pl.BlockSpec((1, LANES), lambda bi, p, i: (0, p))],
        out_specs=pl.BlockSpec((1, t, LANES), lambda bi, p, i: (bi, i, p)),
        out_shape=jax.ShapeDtypeStruct((b, s, n_pairs * LANES), BF16),
        scratch_shapes=[pltpu.VMEM((s, LANES), BF16),
                        pltpu.VMEM((nb, LANES), F32),
                        pltpu.VMEM((nb, LANES, t), BF16),
                        pltpu.VMEM((2, nb, t), F32),
                        pltpu.VMEM((2, HEAD_DIM, t), F32),
                        pltpu.VMEM((2, nb, t, t), F32)],
        compiler_params=_cparams(("arbitrary", "arbitrary", "arbitrary")),
        name="moba_attention",
    )(qk, qk, qkv, bias, cbias, gain)


def _outproj_kernel(osb_ref, omb_ref, x_ref, mod_ref, w1_ref, w2_ref, g_ref, b_ref, o_ref, *, alpha):
    y = _dot(osb_ref[0], w1_ref[...]) + _dot(omb_ref[0], w2_ref[...])
    gate = mod_ref[0, 2:3, :]
    r = alpha * x_ref[0] + gate * y
    o_ref[0] = _layer_norm(r, g_ref[...], b_ref[...])


def _out_projection(o_sb, o_mb, x, mod, w1, w2, ln_g, ln_b, alpha, tm):
    b, s, d = x.shape
    h1, h2 = w1.shape[0], w2.shape[0]
    const = lambda bi, i: (0, 0)
    return pl.pallas_call(
        functools.partial(_outproj_kernel, alpha=alpha),
        grid=(b, s // tm),
        in_specs=[pl.BlockSpec((1, tm, h1), lambda bi, i: (bi, i, 0)),
                  pl.BlockSpec((1, tm, h2), lambda bi, i: (bi, i, 0)),
                  pl.BlockSpec((1, tm, d), lambda bi, i: (bi, i, 0)),
                  pl.BlockSpec((1, 6, d), lambda bi, i: (bi, 0, 0)),
                  pl.BlockSpec((h1, d), const),
                  pl.BlockSpec((h2, d), const),
                  pl.BlockSpec((1, d), const),
                  pl.BlockSpec((1, d), const)],
        out_specs=pl.BlockSpec((1, tm, d), lambda bi, i: (bi, i, 0)),
        out_shape=jax.ShapeDtypeStruct((b, s, d), F32),
        compiler_params=_cparams(("arbitrary", "arbitrary")),
        name="out_proj_ln",
    )(o_sb, o_mb, x, mod, w1, w2, ln_g, ln_b)


def _ffn_kernel(x_ref, mod_ref, wg_ref, wu_ref, wd_ref, g_ref, b_ref, o_ref, u_ref, acc_ref, *, alpha):
    j = pl.program_id(2)

    @pl.when(j == 0)
    def _():
        shift = mod_ref[0, 3:4, :]
        scale = mod_ref[0, 4:5, :]
        u_ref[...] = (x_ref[0] * (1.0 + scale) + shift).astype(BF16)
        acc_ref[...] = jnp.zeros(acc_ref.shape, F32)

    u = u_ref[...]
    hg = _dot(u, wg_ref[...])
    hu = _dot(u, wu_ref[...])
    hidden = (hg * jax.nn.sigmoid(hg) * hu).astype(BF16)
    acc_ref[...] += _dot(hidden, wd_ref[...])

    @pl.when(j == pl.num_programs(2) - 1)
    def _():
        gate = mod_ref[0, 5:6, :]
        r = alpha * x_ref[0] + gate * acc_ref[...]
        o_ref[0] = _layer_norm(r, g_ref[...], b_ref[...])


def _ff_tile(d_ff):
    best = None
    for nf in range(1, d_ff // LANES + 1):
        if d_ff % nf == 0 and (d_ff // nf) % LANES == 0 and d_ff // nf <= 1536:
            best = d_ff // nf
            break
    return best if best is not None else d_ff


def _feed_forward(x, mod, wg, wu, wd, ln_g, ln_b, alpha, tm):
    b, s, d = x.shape
    d_ff = wg.shape[1]
    tf = _ff_tile(d_ff)
    return pl.pallas_call(
        functools.partial(_ffn_kernel, alpha=alpha),
        grid=(b, s // tm, d_ff // tf),
        in_specs=[pl.BlockSpec((1, tm, d), lambda bi, i, j: (bi, i, 0)),
                  pl.BlockSpec((1, 6, d), lambda bi, i, j: (bi, 0, 0)),
                  pl.BlockSpec((d, tf), lambda bi, i, j: (0, j)),
                  pl.BlockSpec((d, tf), lambda bi, i, j: (0, j)),
                  pl.BlockSpec((tf, d), lambda bi, i, j: (j, 0)),
                  pl.BlockSpec((1, d), lambda bi, i, j: (0, 0)),
                  pl.BlockSpec((1, d), lambda bi, i, j: (0, 0))],
        out_specs=pl.BlockSpec((1, tm, d), lambda bi, i, j: (bi, i, 0)),
        out_shape=jax.ShapeDtypeStruct((b, s, d), F32),
        scratch_shapes=[pltpu.VMEM((tm, d), BF16), pltpu.VMEM((tm, d), F32)],
        compiler_params=_cparams(("arbitrary", "arbitrary", "arbitrary")),
        name="ffn_ln",
    )(x, mod, wg, wu, wd, ln_g, ln_b)


def _t5_bucket(dist):
    n = jnp.maximum(dist, 0)
    max_exact = N_BUCKETS // 2
    nf = jnp.maximum(n, 1).astype(F32)
    large = max_exact + (jnp.log(nf / max_exact) / math.log(MAX_DISTANCE / max_exact)
                         * (N_BUCKETS - max_exact)).astype(jnp.int32)
    large = jnp.minimum(large, N_BUCKETS - 1)
    return jnp.where(n < max_exact, n, large)


def _bucket_lookup(rel_h, bucket):
    out = jnp.zeros((rel_h.shape[0],) + bucket.shape, F32)
    for b in range(N_BUCKETS):
        out = jnp.where(bucket[None] == b, rel_h[:, b].reshape((-1,) + (1,) * bucket.ndim), out)
    return out


def _moba_bias_tiles(rel_bias):
    t = ATT_BLOCK
    j = jnp.arange(t)[:, None]
    i = jnp.arange(t)[None, :]
    rel_h = rel_bias.T.astype(F32)
    own = _bucket_lookup(rel_h, _t5_bucket(i - j))
    prev = _bucket_lookup(rel_h, _t5_bucket(t + i - j))
    far = _bucket_lookup(rel_h, _t5_bucket(jnp.full((t,), 2 * t)))
    return jnp.stack([own, prev], axis=1), far


def kernel(x, c, w_ada, b_ada, w_in, sb_gain, mb_gain, w_out, ln1_g, ln1_b,
           w_gate, w_up, w_down, ln2_g, ln2_b, rel_bias):
    depth = w_ada.shape[0]
    b, s, d = x.shape
    d_sb = sb_gain.shape[1]
    d_mb = mb_gain.shape[1]
    assert ATT_BLOCK == MOBA_BLOCK and MOBA_BLOCK >= 2 * MAX_DISTANCE
    assert s % ATT_BLOCK == 0 and (s // ATT_BLOCK) % 8 == 0
    assert d_sb % LANES == 0 and d_mb % LANES == 0 and d % LANES == 0
    alpha = float((2 * depth) ** 0.25)
    tm = min(512, s)
    sb_pairs, mb_pairs = d_sb // LANES, d_mb // LANES

    c_pad = jnp.pad(c, ((0, (-c.shape[0]) % 8), (0, 0)))
    bias_tiles, cbias = _moba_bias_tiles(rel_bias)

    for l in range(depth):
        mod = _modulation(c_pad, w_ada[l], b_ada[l][None, :])[:b].reshape(b, 6, d)
        w = w_in[l]
        w_lo = jnp.concatenate([w[:, :3 * d_sb], w[:, 3 * d_sb + 2 * d_mb:]], axis=1).astype(BF16)
        w_hi = w[:, 3 * d_sb:3 * d_sb + 2 * d_mb]
        w_hh = w_hi.astype(BF16)
        w_hl = (w_hi - w_hh.astype(F32)).astype(BF16)
        qkv, qk_mb = _in_projection(x, mod, w_lo, w_hh, w_hl, tm)

        o_sb = _sb_attention(qkv, sb_gain[l][None, :], sb_pairs, 0, sb_pairs, 2 * sb_pairs)
        o_mb = _moba_attention(qk_mb, qkv, bias_tiles, cbias, mb_gain[l][None, :], mb_pairs,
                               0, mb_pairs, 3 * sb_pairs)

        wo = w_out[l].astype(BF16)
        x = _out_projection(o_sb, o_mb, x, mod, wo[:d_sb], wo[d_sb:], ln1_g[l][None, :], ln1_b[l][None, :],
                            alpha, tm)
        x = _feed_forward(x, mod, w_gate[l].astype(BF16), w_up[l].astype(BF16), w_down[l].astype(BF16),
                          ln2_g[l][None, :], ln2_b[l][None, :], alpha, tm)
    return x
```

```python
import functools
import math

import jax
import jax.numpy as jnp
from jax import lax
from jax.experimental import pallas as pl
from jax.experimental.pallas import tpu as pltpu

HEAD_DIM = 64
MOBA_BLOCK = 256
MOBA_TOPK = 3
N_BUCKETS = 32
MAX_DISTANCE = 128
LN_EPS = 1e-5
RMS_EPS = 1e-6
NEG_BIG = -1e30
SB_DEAD_LOG = 120.0
SB_PARKED = 1e30
LANES = 128
ATT_BLOCK = 256
VMEM_LIMIT = 56 * 1024 * 1024

F32 = jnp.float32
BF16 = jnp.bfloat16


def _cparams(sem):
    return pltpu.CompilerParams(dimension_semantics=sem, vmem_limit_bytes=VMEM_LIMIT)


def _split(a):
    hi = a.astype(BF16)
    lo = (a - hi.astype(F32)).astype(BF16)
    return hi, lo


def _dot(a, b):
    return jnp.dot(a, b, preferred_element_type=F32)


def _dot_nt(a, b):
    return lax.dot_general(a, b, (((1,), (1,)), ((), ())), preferred_element_type=F32)


def _layer_norm(r, g, b):
    mu = jnp.mean(r, axis=-1, keepdims=True)
    d = r - mu
    var = jnp.mean(d * d, axis=-1, keepdims=True)
    return d * lax.rsqrt(var + LN_EPS) * g + b


def _mod_kernel(c_ref, w_ref, b_ref, o_ref):
    c = c_ref[...]
    cond = c * jax.nn.sigmoid(c)
    c_hi, c_lo = _split(cond)
    w_hi, w_lo = _split(w_ref[...])
    o_ref[...] = _dot(c_hi, w_hi) + _dot(c_hi, w_lo) + _dot(c_lo, w_hi) + b_ref[...]


def _modulation(c_pad, w, b):
    rows, d = c_pad.shape
    n = w.shape[1]
    tn = n // 4 if n % (4 * LANES) == 0 else n
    return pl.pallas_call(
        _mod_kernel,
        grid=(n // tn,),
        in_specs=[pl.BlockSpec((rows, d), lambda j: (0, 0)),
                  pl.BlockSpec((d, tn), lambda j: (0, j)),
                  pl.BlockSpec((1, tn), lambda j: (0, j))],
        out_specs=pl.BlockSpec((rows, tn), lambda j: (0, j)),
        out_shape=jax.ShapeDtypeStruct((rows, n), F32),
        compiler_params=_cparams(("arbitrary",)),
        name="adaln_mod",
    )(c_pad, w, b)


def _inproj_kernel(x_ref, mod_ref, wlo_ref, whh_ref, whl_ref, olo_ref, ohi_ref):
    x = x_ref[0]
    shift = mod_ref[0, 0:1, :]
    scale = mod_ref[0, 1:2, :]
    u = x * (1.0 + scale) + shift
    u_hi, u_lo = _split(u)
    olo_ref[0] = _dot(u_hi, wlo_ref[...]).astype(BF16)
    whh = whh_ref[...]
    ohi_ref[0] = _dot(u_hi, whh) + _dot(u_hi, whl_ref[...]) + _dot(u_lo, whh)


def _in_projection(x, mod, w_lo, w_hh, w_hl, tm):
    b, s, d = x.shape
    n_lo, n_hi = w_lo.shape[1], w_hh.shape[1]
    const = lambda bi, i: (0, 0)
    return pl.pallas_call(
        _inproj_kernel,
        grid=(b, s // tm),
        in_specs=[pl.BlockSpec((1, tm, d), lambda bi, i: (bi, i, 0)),
                  pl.BlockSpec((1, 6, d), lambda bi, i: (bi, 0, 0)),
                  pl.BlockSpec((d, n_lo), const),
                  pl.BlockSpec((d, n_hi), const),
                  pl.BlockSpec((d, n_hi), const)],
        out_specs=[pl.BlockSpec((1, tm, n_lo), lambda bi, i: (bi, i, 0)),
                   pl.BlockSpec((1, tm, n_hi), lambda bi, i: (bi, i, 0))],
        out_shape=[jax.ShapeDtypeStruct((b, s, n_lo), BF16),
                   jax.ShapeDtypeStruct((b, s, n_hi), F32)],
        compiler_params=_cparams(("arbitrary", "arbitrary")),
        name="in_proj",
    )(x, mod, w_lo, w_hh, w_hl)


def _store_values_transposed(v_ref, vt_ref):
    t = ATT_BLOCK
    for n in range(vt_ref.shape[0]):
        vt_ref[n] = v_ref[0, n * t:(n + 1) * t, :].astype(F32).T.astype(BF16)


def _head_rms(o0_t, o1_t, gain):
    def norm(o_t):
        ms = jnp.mean(o_t * o_t, axis=0, keepdims=True)
        return o_t * lax.rsqrt(ms + RMS_EPS)
    both = jnp.concatenate([norm(o0_t), norm(o1_t)], axis=0)
    return both.T * gain


def _for_tiles(lo, hi, body, carry):
    cnt = hi - lo
    odd = cnt % 2
    carry = lax.cond(odd == 1, lambda c: body((lo,), c), lambda c: c, carry)
    first = lo + odd
    return lax.fori_loop(0, cnt // 2, lambda i, c: body((first + 2 * i, first + 2 * i + 1), c), carry)


def _masked_heads(q2, scale):
    lane = lax.broadcasted_iota(jnp.int32, q2.shape, 1)
    lo = lane < HEAD_DIM
    return [(jnp.where(lo, q2, 0.0) * scale).astype(BF16),
            (jnp.where(lo, 0.0, q2) * scale).astype(BF16)]


def _sb_kernel(q_ref, k_ref, v_ref, g_ref, o_ref, vt_ref, acc_ref, z_ref, w_ref):
    t = ATT_BLOCK
    gi = pl.program_id(2)

    @pl.when(gi == 0)
    def _():
        _store_values_transposed(v_ref, vt_ref)

    qh = []
    for j in range(2):
        qh += _masked_heads(q_ref[0, j * t:(j + 1) * t, :].astype(F32), HEAD_DIM ** -0.5)
    row = lax.broadcasted_iota(jnp.int32, (t, t), 0)
    col = lax.broadcasted_iota(jnp.int32, (t, t), 1)
    past = row < col
    tri = jnp.where(col >= row, 1.0, 0.0).astype(BF16)
    last_row = lax.broadcasted_iota(jnp.int32, (8, t), 0) == 7

    def sb_round(r, carries, diag):
        n = [2 * gi + j - r for j in range(2)]
        nc = [jnp.maximum(x, 0) for x in n]
        for j in range(2):
            k2 = k_ref[0, pl.ds(pl.multiple_of(nc[j] * t, t), t), :]
            for h in range(2):
                z_ref[2 * j + h] = _dot_nt(k2, qh[2 * j + h])
        out = []
        for s in range(4):
            carry = carries[s] if diag else jnp.where(n[s // 2] >= 0, carries[s], SB_PARKED)
            z = z_ref[s]
            neg_abs = pltpu.bitcast(pltpu.bitcast(z, jnp.uint32) | jnp.uint32(0x80000000), F32)
            a = jnp.maximum(z, 0.0) + jnp.log(1.0 + jnp.exp(neg_abs))
            if diag:
                a = jnp.where(past, a, 0.0)
            a = jnp.concatenate([a[:t - 8], a[t - 8:] + jnp.where(last_row, carry, 0.0)], axis=0)
            ainc = _dot(tri, a.astype(BF16))
            w = jnp.exp(z - ainc)
            if diag:
                w = jnp.where(past, w, 0.0)
            w_ref[s] = w.astype(BF16)
            out.append(ainc[0:1, :])
        for s in range(4):
            h = s % 2
            acc_ref[s] += _dot(vt_ref[nc[s // 2], h * HEAD_DIM:(h + 1) * HEAD_DIM, :], w_ref[s])
        return out

    def any_alive(carries, r_next):
        least = jnp.min(jnp.minimum(jnp.minimum(carries[0], carries[1]), jnp.minimum(carries[2], carries[3])))
        return jnp.logical_and(r_next <= 2 * gi + 1, least < SB_DEAD_LOG).astype(jnp.int32)

    acc_ref[...] = jnp.zeros(acc_ref.shape, F32)
    zero = jnp.zeros((1, t), F32)
    carries = sb_round(0, [zero] * 4, True)

    def body(st):
        r = st[1]
        out = sb_round(r, st[2:], False)
        return (any_alive(out, r + 1), r + 1, *out)

    lax.while_loop(lambda st: st[0] > 0, body, (any_alive(carries, 1), jnp.int32(1), *carries))
    for j in range(2):
        o_ref[0, j * t:(j + 1) * t, :] = _head_rms(acc_ref[2 * j], acc_ref[2 * j + 1], g_ref[...]).astype(o_ref.dtype)


def _sb_attention(qkv, gain, n_pairs, q_blk0, k_blk0, v_blk0):
    b, s, _ = qkv.shape
    t = ATT_BLOCK
    return pl.pallas_call(
        _sb_kernel,
        grid=(b, n_pairs, s // (2 * t)),
        in_specs=[pl.BlockSpec((1, 2 * t, LANES), lambda bi, p, i: (bi, i, q_blk0 + p)),
                  pl.BlockSpec((1, s, LANES), lambda bi, p, i: (bi, 0, k_blk0 + p)),
                  pl.BlockSpec((1, s, LANES), lambda bi, p, i: (bi, 0, v_blk0 + p)),
                  pl.BlockSpec((1, LANES), lambda bi, p, i: (0, p))],
        out_specs=pl.BlockSpec((1, 2 * t, LANES), lambda bi, p, i: (bi, i, p)),
        out_shape=jax.ShapeDtypeStruct((b, s, n_pairs * LANES), BF16),
        scratch_shapes=[pltpu.VMEM((s // t, LANES, t), BF16),
                        pltpu.VMEM((4, HEAD_DIM, t), F32),
                        pltpu.VMEM((4, t, t), F32),
                        pltpu.VMEM((4, t, t), BF16)],
        compiler_params=_cparams(("arbitrary", "arbitrary", "arbitrary")),
        name="sb_attention",
    )(qkv, qkv, qkv, gain)


def _moba_kernel(q_ref, k_ref, v_ref, bias_ref, cb_ref, g_ref, o_ref,
                 kb_ref, km_ref, vt_ref, sel_ref, acc_ref, s_ref):
    t = ATT_BLOCK
    p_id = pl.program_id(1)
    qi = pl.program_id(2)
    nb = k_ref.shape[1] // t

    @pl.when(qi == 0)
    def _():
        kf = k_ref[0]
        kb_ref[...] = kf.astype(BF16)
        km_ref[...] = jnp.mean(kf.reshape(nb, t, LANES), axis=1)
        _store_values_transposed(v_ref, vt_ref)

    q2 = q_ref[0]
    lane = lax.broadcasted_iota(jnp.int32, (t, LANES), 1)
    qs = _masked_heads(q2, HEAD_DIM ** -0.5)
    row = lax.broadcasted_iota(jnp.int32, (t, t), 0)
    col = lax.broadcasted_iota(jnp.int32, (t, t), 1)
    causal = row <= col
    blk = lax.broadcasted_iota(jnp.int32, (nb, t), 0)
    blk_f = blk.astype(F32)
    fully_past = blk < qi
    km_hi, km_lo = _split(km_ref[...])

    for h in range(2):
        qh = jnp.where(lane < HEAD_DIM, q2, 0.0) if h == 0 else jnp.where(lane < HEAD_DIM, 0.0, q2)
        q_hi, q_lo = _split(qh)
        gate = _dot_nt(km_hi, q_hi) + _dot_nt(km_lo, q_hi) + _dot_nt(km_hi, q_lo)
        g = jnp.where(fully_past, gate, -jnp.inf)
        mask = jnp.full((nb, t), NEG_BIG, F32)
        for _ in range(MOBA_TOPK):
            mx = jnp.max(g, axis=0, keepdims=True)
            first = jnp.min(jnp.where(g == mx, blk_f, float(nb)), axis=0, keepdims=True)
            pick = blk_f == first
            mask = jnp.where(pick, 0.0, mask)
            g = jnp.where(pick, -jnp.inf, g)
        sel_ref[h] = jnp.where(fully_past, mask, NEG_BIG)

    def scores(n, h):
        start = pl.multiple_of(n * t, t)
        return _dot_nt(kb_ref[pl.ds(start, t), :], qs[h])

    mx = []
    for h in range(2):
        s = jnp.where(causal, scores(qi, h) + bias_ref[h, 0], NEG_BIG)
        s_ref[h, qi] = s
        mx.append(jnp.max(s, axis=0, keepdims=True))

    def prev_block(m):
        n = qi - 1
        out = []
        for h in range(2):
            s = scores(n, h) + bias_ref[h, 1] + sel_ref[h, pl.ds(n, 1), :]
            s_ref[h, n] = s
            out.append(jnp.maximum(m[h], jnp.max(s, axis=0, keepdims=True)))
        return tuple(out)

    mx = lax.cond(qi > 0, prev_block, lambda m: m, tuple(mx))

    def far_blocks(tiles, m):
        ss = [[scores(n, h) for h in range(2)] for n in tiles]
        out = list(m)
        for j, n in enumerate(tiles):
            for h in range(2):
                cb = cb_ref[pl.ds(2 * p_id + h, 1), :]
                s = ss[j][h] + (sel_ref[h, pl.ds(n, 1), :] + cb)
                s_ref[h, n] = s
                out[h] = jnp.maximum(out[h], jnp.max(s, axis=0, keepdims=True))
        return tuple(out)

    mx = _for_tiles(0, jnp.maximum(qi - 1, 0), far_blocks, mx)

    def weigh(tiles, l):
        out = list(l)
        for n in tiles:
            for h in range(2):
                p = jnp.exp(s_ref[h, n] - mx[h])
                out[h] = out[h] + jnp.sum(p, axis=0, keepdims=True)
                acc_ref[h] += _dot(vt_ref[n, h * HEAD_DIM:(h + 1) * HEAD_DIM, :], p.astype(BF16))
        return tuple(out)

    acc_ref[...] = jnp.zeros(acc_ref.shape, F32)
    zero = jnp.zeros((1, t), F32)
    l = _for_tiles(0, qi + 1, weigh, (zero, zero))
    o_ref[0] = _head_rms(acc_ref[0] / l[0], acc_ref[1] / l[1], g_ref[...]).astype(o_ref.dtype)


def _moba_attention(qk, qkv, bias, cbias, gain, n_pairs, q_blk0, k_blk0, v_blk0):
    b, s, _ = qk.shape
    t = ATT_BLOCK
    nb = s // t
    n_heads = bias.shape[0]
    return pl.pallas_call(
        _moba_kernel,
        grid=(b, n_pairs, nb),
        in_specs=[pl.BlockSpec((1, t, LANES), lambda bi, p, i: (bi, i, q_blk0 + p)),
                  pl.BlockSpec((1, s, LANES), lambda bi, p, i: (bi, 0, k_blk0 + p)),
                  pl.BlockSpec((1, s, LANES), lambda bi, p, i: (bi, 0, v_blk0 + p)),
                  pl.BlockSpec((2, 2, t, t), lambda bi, p, i: (p, 0, 0, 0)),
                  pl.BlockSpec((n_heads, t), lambda bi, p, i: (0, 0)),
                  pl.BlockSpec((1, LANES), lambda bi, p, i: (0, p))],
        out_specs=pl.BlockSpec((1, t, LANES), lambda bi, p, i: (bi, i, p)),
        out_shape=jax.ShapeDtypeStruct((b, s, n_pairs * LANES), BF16),
        scratch_shapes=[pltpu.VMEM((s, LANES), BF16),
                        pltpu.VMEM((nb, LANES), F32),
                        pltpu.VMEM((nb, LANES, t), BF16),
                        pltpu.VMEM((2, nb, t), F32),
                        pltpu.VMEM((2, HEAD_DIM, t), F32),
                        pltpu.VMEM((2, nb, t, t), F32)],
        compiler_params=_cparams(("arbitrary", "arbitrary", "arbitrary")),
        name="moba_attention",
    )(qk, qk, qkv, bias, cbias, gain)


def _outproj_kernel(osb_ref, omb_ref, x_ref, mod_ref, w1_ref, w2_ref, g_ref, b_ref, o_ref, *, alpha):
    y = _dot(osb_ref[0], w1_ref[...]) + _dot(omb_ref[0], w2_ref[...])
    gate = mod_ref[0, 2:3, :]
    r = alpha * x_ref[0] + gate * y
    o_ref[0] = _layer_norm(r, g_ref[...], b_ref[...])


def _out_projection(o_sb, o_mb, x, mod, w1, w2, ln_g, ln_b, alpha, tm):
    b, s, d = x.shape
    h1, h2 = w1.shape[0], w2.shape[0]
    const = lambda bi, i: (0, 0)
    return pl.pallas_call(
        functools.partial(_outproj_kernel, alpha=alpha),
        grid=(b, s // tm),
        in_specs=[pl.BlockSpec((1, tm, h1), lambda bi, i: (bi, i, 0)),
                  pl.BlockSpec((1, tm, h2), lambda bi, i: (bi, i, 0)),
                  pl.BlockSpec((1, tm, d), lambda bi, i: (bi, i, 0)),
                  pl.BlockSpec((1, 6, d), lambda bi, i: (bi, 0, 0)),
                  pl.BlockSpec((h1, d), const),
                  pl.BlockSpec((h2, d), const),
                  pl.BlockSpec((1, d), const),
                  pl.BlockSpec((1, d), const)],
        out_specs=pl.BlockSpec((1, tm, d), lambda bi, i: (bi, i, 0)),
        out_shape=jax.ShapeDtypeStruct((b, s, d), F32),
        compiler_params=_cparams(("arbitrary", "arbitrary")),
        name="out_proj_ln",
    )(o_sb, o_mb, x, mod, w1, w2, ln_g, ln_b)


def _ffn_kernel(x_ref, mod_ref, wg_ref, wu_ref, wd_ref, g_ref, b_ref, o_ref, u_ref, acc_ref, *, alpha):
    j = pl.program_id(2)

    @pl.when(j == 0)
    def _():
        shift = mod_ref[0, 3:4, :]
        scale = mod_ref[0, 4:5, :]
        u_ref[...] = (x_ref[0] * (1.0 + scale) + shift).astype(BF16)
        acc_ref[...] = jnp.zeros(acc_ref.shape, F32)

    u = u_ref[...]
    hg = _dot(u, wg_ref[...])
    hu = _dot(u, wu_ref[...])
    hidden = (hg * jax.nn.sigmoid(hg) * hu).astype(BF16)
    acc_ref[...] += _dot(hidden, wd_ref[...])

    @pl.when(j == pl.num_programs(2) - 1)
    def _():
        gate = mod_ref[0, 5:6, :]
        r = alpha * x_ref[0] + gate * acc_ref[...]
        o_ref[0] = _layer_norm(r, g_ref[...], b_ref[...])


def _ff_tile(d_ff):
    best = None
    for nf in range(1, d_ff // LANES + 1):
        if d_ff % nf == 0 and (d_ff // nf) % LANES == 0 and d_ff // nf <= 1536:
            best = d_ff // nf
            break
    return best if best is not None else d_ff


def _feed_forward(x, mod, wg, wu, wd, ln_g, ln_b, alpha, tm):
    b, s, d = x.shape
    d_ff = wg.shape[1]
    tf = _ff_tile(d_ff)
    return pl.pallas_call(
        functools.partial(_ffn_kernel, alpha=alpha),
        grid=(b, s // tm, d_ff // tf),
        in_specs=[pl.BlockSpec((1, tm, d), lambda bi, i, j: (bi, i, 0)),
                  pl.BlockSpec((1, 6, d), lambda bi, i, j: (bi, 0, 0)),
                  pl.BlockSpec((d, tf), lambda bi, i, j: (0, j)),
                  pl.BlockSpec((d, tf), lambda bi, i, j: (0, j)),
                  pl.BlockSpec((tf, d), lambda bi, i, j: (j, 0)),
                  pl.BlockSpec((1, d), lambda bi, i, j: (0, 0)),
                  pl.BlockSpec((1, d), lambda bi, i, j: (0, 0))],
        out_specs=pl.BlockSpec((1, tm, d), lambda bi, i, j: (bi, i, 0)),
        out_shape=jax.ShapeDtypeStruct((b, s, d), F32),
        scratch_shapes=[pltpu.VMEM((tm, d), BF16), pltpu.VMEM((tm, d), F32)],
        compiler_params=_cparams(("arbitrary", "arbitrary", "arbitrary")),
        name="ffn_ln",
    )(x, mod, wg, wu, wd, ln_g, ln_b)


def _t5_bucket(dist):
    n = jnp.maximum(dist, 0)
    max_exact = N_BUCKETS // 2
    nf = jnp.maximum(n, 1).astype(F32)
    large = max_exact + (jnp.log(nf / max_exact) / math.log(MAX_DISTANCE / max_exact)
                         * (N_BUCKETS - max_exact)).astype(jnp.int32)
    large = jnp.minimum(large, N_BUCKETS - 1)
    return jnp.where(n < max_exact, n, large)


def _bucket_lookup(rel_h, bucket):
    out = jnp.zeros((rel_h.shape[0],) + bucket.shape, F32)
    for b in range(N_BUCKETS):
        out = jnp.where(bucket[None] == b, rel_h[:, b].reshape((-1,) + (1,) * bucket.ndim), out)
    return out


def _moba_bias_tiles(rel_bias):
    t = ATT_BLOCK
    j = jnp.arange(t)[:, None]
    i = jnp.arange(t)[None, :]
    rel_h = rel_bias.T.astype(F32)
    own = _bucket_lookup(rel_h, _t5_bucket(i - j))
    prev = _bucket_lookup(rel_h, _t5_bucket(t + i - j))
    far = _bucket_lookup(rel_h, _t5_bucket(jnp.full((t,), 2 * t)))
    return jnp.stack([own, prev], axis=1), far


def kernel(x, c, w_ada, b_ada, w_in, sb_gain, mb_gain, w_out, ln1_g, ln1_b,
           w_gate, w_up, w_down, ln2_g, ln2_b, rel_bias):
    depth = w_ada.shape[0]
    b, s, d = x.shape
    d_sb = sb_gain.shape[1]
    d_mb = mb_gain.shape[1]
    assert ATT_BLOCK == MOBA_BLOCK and MOBA_BLOCK >= 2 * MAX_DISTANCE
    assert s % ATT_BLOCK == 0 and (s // ATT_BLOCK) % 8 == 0
    assert d_sb % LANES == 0 and d_mb % LANES == 0 and d % LANES == 0
    alpha = float((2 * depth) ** 0.25)
    tm = min(512, s)
    sb_pairs, mb_pairs = d_sb // LANES, d_mb // LANES

    c_pad = jnp.pad(c, ((0, (-c.shape[0]) % 8), (0, 0)))
    bias_tiles, cbias = _moba_bias_tiles(rel_bias)

    for l in range(depth):
        mod = _modulation(c_pad, w_ada[l], b_ada[l][None, :])[:b].reshape(b, 6, d)
        w = w_in[l]
        w_lo = jnp.concatenate([w[:, :3 * d_sb], w[:, 3 * d_sb + 2 * d_mb:]], axis=1).astype(BF16)
        w_hi = w[:, 3 * d_sb:3 * d_sb + 2 * d_mb]
        w_hh = w_hi.astype(BF16)
        w_hl = (w_hi - w_hh.astype(F32)).astype(BF16)
        qkv, qk_mb = _in_projection(x, mod, w_lo, w_hh, w_hl, tm)

        o_sb = _sb_attention(qkv, sb_gain[l][None, :], sb_pairs, 0, sb_pairs, 2 * sb_pairs)
        o_mb = _moba_attention(qk_mb, qkv, bias_tiles, cbias, mb_gain[l][None, :], mb_pairs,
                               0, mb_pairs, 3 * sb_pairs)

        wo = w_out[l].astype(BF16)
        x = _out_projection(o_sb, o_mb, x, mod, wo[:d_sb], wo[d_sb:], ln1_g[l][None, :], ln1_b[l][None, :],
                            alpha, tm)
        x = _feed_forward(x, mod, w_gate[l].astype(BF16), w_up[l].astype(BF16), w_down[l].astype(BF16),
                          ln2_g[l][None, :], ln2_b[l][None, :], alpha, tm)
    return x
```

```python
import functools
import math

import jax
import jax.numpy as jnp
from jax import lax
from jax.experimental import pallas as pl
from jax.experimental.pallas import tpu as pltpu

HEAD_DIM = 64
MOBA_BLOCK = 256
MOBA_TOPK = 3
N_BUCKETS = 32
MAX_DISTANCE = 128
LN_EPS = 1e-5
RMS_EPS = 1e-6
NEG_BIG = -1e30
SB_DEAD_LOG = 120.0
LOG2_E = 1.4426950408889634
SB_PARKED = 1e30
LANES = 128
ATT_BLOCK = 256
VMEM_LIMIT = 56 * 1024 * 1024

F32 = jnp.float32
BF16 = jnp.bfloat16


def _cparams(sem):
    return pltpu.CompilerParams(dimension_semantics=sem, vmem_limit_bytes=VMEM_LIMIT)


def _split(a):
    hi = a.astype(BF16)
    lo = (a - hi.astype(F32)).astype(BF16)
    return hi, lo


def _dot(a, b):
    return jnp.dot(a, b, preferred_element_type=F32)


def _dot_nt(a, b):
    return lax.dot_general(a, b, (((1,), (1,)), ((), ())), preferred_element_type=F32)


def _layer_norm(r, g, b):
    mu = jnp.mean(r, axis=-1, keepdims=True)
    d = r - mu
    var = jnp.mean(d * d, axis=-1, keepdims=True)
    return d * lax.rsqrt(var + LN_EPS) * g + b


def _mod_kernel(c_ref, w_ref, b_ref, o_ref):
    c = c_ref[...]
    cond = c * jax.nn.sigmoid(c)
    c_hi, c_lo = _split(cond)
    w_hi, w_lo = _split(w_ref[...])
    o_ref[...] = _dot(c_hi, w_hi) + _dot(c_hi, w_lo) + _dot(c_lo, w_hi) + b_ref[...]


def _modulation(c_pad, w, b):
    rows, d = c_pad.shape
    n = w.shape[1]
    tn = n // 4 if n % (4 * LANES) == 0 else n
    return pl.pallas_call(
        _mod_kernel,
        grid=(n // tn,),
        in_specs=[pl.BlockSpec((rows, d), lambda j: (0, 0)),
                  pl.BlockSpec((d, tn), lambda j: (0, j)),
                  pl.BlockSpec((1, tn), lambda j: (0, j))],
        out_specs=pl.BlockSpec((rows, tn), lambda j: (0, j)),
        out_shape=jax.ShapeDtypeStruct((rows, n), F32),
        compiler_params=_cparams(("arbitrary",)),
        name="adaln_mod",
    )(c_pad, w, b)


def _inproj_kernel(x_ref, mod_ref, wlo_ref, whh_ref, whl_ref, olo_ref, ohi_ref):
    x = x_ref[0]
    shift = mod_ref[0, 0:1, :]
    scale = mod_ref[0, 1:2, :]
    u = x * (1.0 + scale) + shift
    u_hi, u_lo = _split(u)
    olo_ref[0] = _dot(u_hi, wlo_ref[...]).astype(BF16)
    whh = whh_ref[...]
    ohi_ref[0] = _dot(u_hi, whh) + _dot(u_hi, whl_ref[...]) + _dot(u_lo, whh)


def _in_projection(x, mod, w_lo, w_hh, w_hl, tm):
    b, s, d = x.shape
    n_lo, n_hi = w_lo.shape[1], w_hh.shape[1]
    const = lambda bi, i: (0, 0)
    return pl.pallas_call(
        _inproj_kernel,
        grid=(b, s // tm),
        in_specs=[pl.BlockSpec((1, tm, d), lambda bi, i: (bi, i, 0)),
                  pl.BlockSpec((1, 6, d), lambda bi, i: (bi, 0, 0)),
                  pl.BlockSpec((d, n_lo), const),
                  pl.BlockSpec((d, n_hi), const),
                  pl.BlockSpec((d, n_hi), const)],
        out_specs=[pl.BlockSpec((1, tm, n_lo), lambda bi, i: (bi, i, 0)),
                   pl.BlockSpec((1, tm, n_hi), lambda bi, i: (bi, i, 0))],
        out_shape=[jax.ShapeDtypeStruct((b, s, n_lo), BF16),
                   jax.ShapeDtypeStruct((b, s, n_hi), F32)],
        compiler_params=_cparams(("arbitrary", "arbitrary")),
        name="in_proj",
    )(x, mod, w_lo, w_hh, w_hl)


def _store_values_transposed(v_ref, vt_ref):
    t = ATT_BLOCK
    for n in range(vt_ref.shape[0]):
        vt_ref[n] = v_ref[0, n * t:(n + 1) * t, :].astype(F32).T.astype(BF16)


def _head_rms(o0_t, o1_t, gain):
    def norm(o_t):
        ms = jnp.mean(o_t * o_t, axis=0, keepdims=True)
        return o_t * lax.rsqrt(ms + RMS_EPS)
    both = jnp.concatenate([norm(o0_t), norm(o1_t)], axis=0)
    return both.T * gain


def _masked_heads(q2, scale):
    lane = lax.broadcasted_iota(jnp.int32, q2.shape, 1)
    lo = lane < HEAD_DIM
    return [(jnp.where(lo, q2, 0.0) * scale).astype(BF16),
            (jnp.where(lo, 0.0, q2) * scale).astype(BF16)]


def _sb_kernel(q_ref, k_ref, v_ref, g_ref, o_ref, vt_ref, acc_ref, z_ref, w_ref):
    t = ATT_BLOCK
    gi = pl.program_id(2)

    @pl.when(gi == 0)
    def _():
        _store_values_transposed(v_ref, vt_ref)

    qh = []
    for j in range(2):
        qh += _masked_heads(q_ref[0, j * t:(j + 1) * t, :].astype(F32), HEAD_DIM ** -0.5)
    row = lax.broadcasted_iota(jnp.int32, (t, t), 0)
    col = lax.broadcasted_iota(jnp.int32, (t, t), 1)
    past = row < col
    tri = jnp.where(col >= row, 1.0, 0.0).astype(BF16)
    last_row = lax.broadcasted_iota(jnp.int32, (8, t), 0) == 7

    def key_block(r, j):
        return jnp.maximum(2 * gi + j - r, 0)

    def logits(r):
        for j in range(2):
            k2 = k_ref[0, pl.ds(pl.multiple_of(key_block(r, j) * t, t), t), :]
            for h in range(2):
                z_ref[r % 2, 2 * j + h] = _dot_nt(k2, qh[2 * j + h])

    def weights(r, carries, diag):
        out = []
        for s in range(4):
            carry = carries[s] if diag else jnp.where(2 * gi + s // 2 - r >= 0, carries[s], SB_PARKED)
            z = z_ref[r % 2, s]
            neg_abs = pltpu.bitcast(pltpu.bitcast(z, jnp.uint32) | jnp.uint32(0x80000000), F32)
            a = jnp.maximum(z, 0.0) + jnp.log(1.0 + jnp.exp(neg_abs))
            if diag:
                a = jnp.where(past, a, 0.0)
            a = jnp.concatenate([a[:t - 8], a[t - 8:] + jnp.where(last_row, carry, 0.0)], axis=0)
            ainc = _dot(tri, a.astype(BF16))
            w = jnp.exp(z - ainc)
            if diag:
                w = jnp.where(past, w, 0.0)
            w_ref[r % 2, s] = w.astype(BF16)
            out.append(ainc[0:1, :])
        return out

    def values(r):
        for s in range(4):
            h = s % 2
            vt = vt_ref[key_block(r, s // 2), h * HEAD_DIM:(h + 1) * HEAD_DIM, :]
            acc_ref[s] += _dot(vt, w_ref[r % 2, s])

    def any_alive(carries, r_next):
        least = jnp.min(jnp.minimum(jnp.minimum(carries[0], carries[1]), jnp.minimum(carries[2], carries[3])))
        return jnp.logical_and(r_next <= 2 * gi + 1, least < SB_DEAD_LOG).astype(jnp.int32)

    acc_ref[...] = jnp.zeros(acc_ref.shape, F32)
    zero = jnp.zeros((1, t), F32)
    logits(0)
    logits(1)
    carries = weights(0, [zero] * 4, True)

    def body(st):
        r = st[1]
        values(r - 1)
        out = weights(r, st[2:], False)
        logits(r + 1)
        return (any_alive(out, r + 1), r + 1, *out)

    st = lax.while_loop(lambda st: st[0] > 0, body, (any_alive(carries, 1), jnp.int32(1), *carries))
    values(st[1] - 1)
    for j in range(2):
        o_ref[0, j * t:(j + 1) * t, :] = _head_rms(acc_ref[2 * j], acc_ref[2 * j + 1], g_ref[...]).astype(o_ref.dtype)


def _sb_attention(qkv, gain, n_pairs, q_blk0, k_blk0, v_blk0):
    b, s, _ = qkv.shape
    t = ATT_BLOCK
    return pl.pallas_call(
        _sb_kernel,
        grid=(b, n_pairs, s // (2 * t)),
        in_specs=[pl.BlockSpec((1, 2 * t, LANES), lambda bi, p, i: (bi, i, q_blk0 + p)),
                  pl.BlockSpec((1, s, LANES), lambda bi, p, i: (bi, 0, k_blk0 + p)),
                  pl.BlockSpec((1, s, LANES), lambda bi, p, i: (bi, 0, v_blk0 + p)),
                  pl.BlockSpec((1, LANES), lambda bi, p, i: (0, p))],
        out_specs=pl.BlockSpec((1, 2 * t, LANES), lambda bi, p, i: (bi, i, p)),
        out_shape=jax.ShapeDtypeStruct((b, s, n_pairs * LANES), BF16),
        scratch_shapes=[pltpu.VMEM((s // t, LANES, t), BF16),
                        pltpu.VMEM((4, HEAD_DIM, t), F32),
                        pltpu.VMEM((2, 4, t, t), F32),
                        pltpu.VMEM((2, 4, t, t), BF16)],
        compiler_params=_cparams(("arbitrary", "arbitrary", "arbitrary")),
        name="sb_attention",
    )(qkv, qkv, qkv, gain)


def _moba_gate_kernel(q_ref, k_ref, sel_ref, qb_ref, kb_ref):
    t = ATT_BLOCK
    s_len = k_ref.shape[1]
    nb = s_len // t
    kf = k_ref[0]
    q2 = q_ref[0]
    kb_ref[0] = kf.astype(BF16)
    qb_ref[0] = (q2 * (HEAD_DIM ** -0.5 * LOG2_E)).astype(BF16)
    km = jnp.mean(kf.reshape(nb, t, LANES), axis=1)
    km_hi, km_lo = _split(km)
    km_both = jnp.concatenate([km_hi, km_lo], axis=0)
    lane = lax.broadcasted_iota(jnp.int32, (s_len, LANES), 1)
    blk = lax.broadcasted_iota(jnp.int32, (nb, s_len), 0)
    blk_f = blk.astype(F32)
    q_blk = lax.broadcasted_iota(jnp.int32, (nb, s_len), 1) // t
    fully_past = blk < q_blk
    for h in range(2):
        qh = jnp.where(lane < HEAD_DIM, q2, 0.0) if h == 0 else jnp.where(lane < HEAD_DIM, 0.0, q2)
        q_hi, q_lo = _split(qh)
        hh_lh = _dot_nt(km_both, q_hi)
        gate = hh_lh[:nb] + hh_lh[nb:] + _dot_nt(km_hi, q_lo)
        g = jnp.where(fully_past, gate, -jnp.inf)
        mask = jnp.full((nb, s_len), NEG_BIG, F32)
        for _ in range(MOBA_TOPK):
            mx = jnp.max(g, axis=0, keepdims=True)
            first = jnp.min(jnp.where(g == mx, blk_f, float(nb)), axis=0, keepdims=True)
            pick = blk_f == first
            mask = jnp.where(pick, 0.0, mask)
            g = jnp.where(pick, -jnp.inf, g)
        mask = jnp.where(fully_past, mask, NEG_BIG)
        sel_ref[0, 0, h] = jnp.where(blk == q_blk, 0.0, mask)


def _moba_gate(qk, n_pairs):
    b, s, _ = qk.shape
    nb = s // ATT_BLOCK
    return pl.pallas_call(
        _moba_gate_kernel,
        grid=(b, n_pairs),
        in_specs=[pl.BlockSpec((1, s, LANES), lambda bi, p: (bi, 0, p)),
                  pl.BlockSpec((1, s, LANES), lambda bi, p: (bi, 0, n_pairs + p))],
        out_specs=[pl.BlockSpec((1, 1, 2, nb, s), lambda bi, p: (bi, p, 0, 0, 0)),
                   pl.BlockSpec((1, s, LANES), lambda bi, p: (bi, 0, p)),
                   pl.BlockSpec((1, s, LANES), lambda bi, p: (bi, 0, p))],
        out_shape=[jax.ShapeDtypeStruct((b, n_pairs, 2, nb, s), F32),
                   jax.ShapeDtypeStruct((b, s, n_pairs * LANES), BF16),
                   jax.ShapeDtypeStruct((b, s, n_pairs * LANES), BF16)],
        compiler_params=_cparams(("arbitrary", "arbitrary")),
        name="moba_gate",
    )(qk, qk)


def _moba_kernel(q_ref, k_ref, v_ref, sel_ref, bias_ref, g_ref, o_ref, vt_ref, s_ref, acc_ref):
    t = ATT_BLOCK
    gi = pl.program_id(2)

    @pl.when(gi == 0)
    def _():
        _store_values_transposed(v_ref, vt_ref)

    qs = []
    for j in range(2):
        qs += _masked_heads(q_ref[0, j * t:(j + 1) * t, :].astype(F32), 1.0)

    def tile_kind(n, j):
        return jnp.clip(n - (2 * gi + j) + 2, 0, 3)

    def score_pair(i, mx):
        mx = list(mx)
        for n in (2 * i, 2 * i + 1):
            k2 = k_ref[0, pl.ds(pl.multiple_of(n * t, t), t), :]
            for slot in range(4):
                j, h = slot // 2, slot % 2
                s = _dot_nt(k2, qs[slot]) + bias_ref[h, tile_kind(n, j)]
                s = s + sel_ref[0, 0, h, pl.ds(n, 1), j * t:(j + 1) * t]
                s_ref[slot, n] = s
                mx[slot] = jnp.maximum(mx[slot], jnp.max(s, axis=0, keepdims=True))
        return tuple(mx)

    mx = lax.fori_loop(0, gi + 1, score_pair, (jnp.full((1, t), -jnp.inf, F32),) * 4)

    def weigh_pair(i, l):
        l = list(l)
        for n in (2 * i, 2 * i + 1):
            for slot in range(4):
                h = slot % 2
                p = jnp.exp2(s_ref[slot, n] - mx[slot])
                l[slot] = l[slot] + jnp.sum(p, axis=0, keepdims=True)
                acc_ref[slot] += _dot(vt_ref[n, h * HEAD_DIM:(h + 1) * HEAD_DIM, :], p.astype(BF16))
        return tuple(l)

    acc_ref[...] = jnp.zeros(acc_ref.shape, F32)
    l = lax.fori_loop(0, gi + 1, weigh_pair, (jnp.zeros((1, t), F32),) * 4)
    for j in range(2):
        out = _head_rms(acc_ref[2 * j] / l[2 * j], acc_ref[2 * j + 1] / l[2 * j + 1], g_ref[...])
        o_ref[0, j * t:(j + 1) * t, :] = out.astype(o_ref.dtype)


def _moba_attention(qb, kb, qkv, sel, bias, gain, n_pairs, v_blk0):
    b, s, _ = qb.shape
    t = ATT_BLOCK
    nb = s // t
    return pl.pallas_call(
        _moba_kernel,
        grid=(b, n_pairs, nb // 2),
        in_specs=[pl.BlockSpec((1, 2 * t, LANES), lambda bi, p, i: (bi, i, p)),
                  pl.BlockSpec((1, s, LANES), lambda bi, p, i: (bi, 0, p)),
                  pl.BlockSpec((1, s, LANES), lambda bi, p, i: (bi, 0, v_blk0 + p)),
                  pl.BlockSpec((1, 1, 2, nb, 2 * t), lambda bi, p, i: (bi, p, 0, 0, i)),
                  pl.BlockSpec((2, 4, t, t), lambda bi, p, i: (p, 0, 0, 0)),
                  pl.BlockSpec((1, LANES), lambda bi, p, i: (0, p))],
        out_specs=pl.BlockSpec((1, 2 * t, LANES), lambda bi, p, i: (bi, i, p)),
        out_shape=jax.ShapeDtypeStruct((b, s, n_pairs * LANES), BF16),
        scratch_shapes=[pltpu.VMEM((nb, LANES, t), BF16),
                        pltpu.VMEM((4, nb, t, t), F32),
                        pltpu.VMEM((4, HEAD_DIM, t), F32)],
        compiler_params=_cparams(("arbitrary", "arbitrary", "arbitrary")),
        name="moba_attention",
    )(qb, kb, qkv, sel, bias, gain)


def _outproj_kernel(osb_ref, omb_ref, x_ref, mod_ref, w1_ref, w2_ref, g_ref, b_ref, o_ref, *, alpha):
    y = _dot(osb_ref[0], w1_ref[...]) + _dot(omb_ref[0], w2_ref[...])
    gate = mod_ref[0, 2:3, :]
    r = alpha * x_ref[0] + gate * y
    o_ref[0] = _layer_norm(r, g_ref[...], b_ref[...])


def _out_projection(o_sb, o_mb, x, mod, w1, w2, ln_g, ln_b, alpha, tm):
    b, s, d = x.shape
    h1, h2 = w1.shape[0], w2.shape[0]
    const = lambda bi, i: (0, 0)
    return pl.pallas_call(
        functools.partial(_outproj_kernel, alpha=alpha),
        grid=(b, s // tm),
        in_specs=[pl.BlockSpec((1, tm, h1), lambda bi, i: (bi, i, 0)),
                  pl.BlockSpec((1, tm, h2), lambda bi, i: (bi, i, 0)),
                  pl.BlockSpec((1, tm, d), lambda bi, i: (bi, i, 0)),
                  pl.BlockSpec((1, 6, d), lambda bi, i: (bi, 0, 0)),
                  pl.BlockSpec((h1, d), const),
                  pl.BlockSpec((h2, d), const),
                  pl.BlockSpec((1, d), const),
                  pl.BlockSpec((1, d), const)],
        out_specs=pl.BlockSpec((1, tm, d), lambda bi, i: (bi, i, 0)),
        out_shape=jax.ShapeDtypeStruct((b, s, d), F32),
        compiler_params=_cparams(("arbitrary", "arbitrary")),
        name="out_proj_ln",
    )(o_sb, o_mb, x, mod, w1, w2, ln_g, ln_b)


def _ffn_kernel(x_ref, mod_ref, wg_ref, wu_ref, wd_ref, g_ref, b_ref, o_ref, u_ref, acc_ref, *, alpha):
    j = pl.program_id(2)

    @pl.when(j == 0)
    def _():
        shift = mod_ref[0, 3:4, :]
        scale = mod_ref[0, 4:5, :]
        u_ref[...] = (x_ref[0] * (1.0 + scale) + shift).astype(BF16)
        acc_ref[...] = jnp.zeros(acc_ref.shape, F32)

    u = u_ref[...]
    hg = _dot(u, wg_ref[...])
    hu = _dot(u, wu_ref[...])
    hidden = (hg * jax.nn.sigmoid(hg) * hu).astype(BF16)
    acc_ref[...] += _dot(hidden, wd_ref[...])

    @pl.when(j == pl.num_programs(2) - 1)
    def _():
        gate = mod_ref[0, 5:6, :]
        r = alpha * x_ref[0] + gate * acc_ref[...]
        o_ref[0] = _layer_norm(r, g_ref[...], b_ref[...])


def _ff_tile(d_ff):
    best = None
    for nf in range(1, d_ff // LANES + 1):
        if d_ff % nf == 0 and (d_ff // nf) % LANES == 0 and d_ff // nf <= 1536:
            best = d_ff // nf
            break
    return best if best is not None else d_ff


def _feed_forward(x, mod, wg, wu, wd, ln_g, ln_b, alpha, tm):
    b, s, d = x.shape
    d_ff = wg.shape[1]
    tf = _ff_tile(d_ff)
    return pl.pallas_call(
        functools.partial(_ffn_kernel, alpha=alpha),
        grid=(b, s // tm, d_ff // tf),
        in_specs=[pl.BlockSpec((1, tm, d), lambda bi, i, j: (bi, i, 0)),
                  pl.BlockSpec((1, 6, d), lambda bi, i, j: (bi, 0, 0)),
                  pl.BlockSpec((d, tf), lambda bi, i, j: (0, j)),
                  pl.BlockSpec((d, tf), lambda bi, i, j: (0, j)),
                  pl.BlockSpec((tf, d), lambda bi, i, j: (j, 0)),
                  pl.BlockSpec((1, d), lambda bi, i, j: (0, 0)),
                  pl.BlockSpec((1, d), lambda bi, i, j: (0, 0))],
        out_specs=pl.BlockSpec((1, tm, d), lambda bi, i, j: (bi, i, 0)),
        out_shape=jax.ShapeDtypeStruct((b, s, d), F32),
        scratch_shapes=[pltpu.VMEM((tm, d), BF16), pltpu.VMEM((tm, d), F32)],
        compiler_params=_cparams(("arbitrary", "arbitrary", "arbitrary")),
        name="ffn_ln",
    )(x, mod, wg, wu, wd, ln_g, ln_b)


def _t5_bucket(dist):
    n = jnp.maximum(dist, 0)
    max_exact = N_BUCKETS // 2
    nf = jnp.maximum(n, 1).astype(F32)
    large = max_exact + (jnp.log(nf / max_exact) / math.log(MAX_DISTANCE / max_exact)
                         * (N_BUCKETS - max_exact)).astype(jnp.int32)
    large = jnp.minimum(large, N_BUCKETS - 1)
    return jnp.where(n < max_exact, n, large)


def _bucket_lookup(rel_h, bucket):
    out = jnp.zeros((rel_h.shape[0],) + bucket.shape, F32)
    for b in range(N_BUCKETS):
        out = jnp.where(bucket[None] == b, rel_h[:, b].reshape((-1,) + (1,) * bucket.ndim), out)
    return out


def _moba_bias_tiles(rel_bias):
    t = ATT_BLOCK
    j = jnp.arange(t)[:, None]
    i = jnp.arange(t)[None, :]
    rel_h = rel_bias.T.astype(F32) * LOG2_E
    far = _bucket_lookup(rel_h, _t5_bucket(jnp.full((t, t), 2 * t)))
    prev = _bucket_lookup(rel_h, _t5_bucket(t + i - j))
    own = jnp.where((j <= i)[None], _bucket_lookup(rel_h, _t5_bucket(i - j)), NEG_BIG)
    return jnp.stack([far, prev, own, jnp.full_like(far, NEG_BIG)], axis=1)


def kernel(x, c, w_ada, b_ada, w_in, sb_gain, mb_gain, w_out, ln1_g, ln1_b,
           w_gate, w_up, w_down, ln2_g, ln2_b, rel_bias):
    depth = w_ada.shape[0]
    b, s, d = x.shape
    d_sb = sb_gain.shape[1]
    d_mb = mb_gain.shape[1]
    assert ATT_BLOCK == MOBA_BLOCK and MOBA_BLOCK >= 2 * MAX_DISTANCE
    assert s % ATT_BLOCK == 0 and (s // ATT_BLOCK) % 8 == 0
    assert d_sb % LANES == 0 and d_mb % LANES == 0 and d % LANES == 0
    alpha = float((2 * depth) ** 0.25)
    tm = min(512, s)
    sb_pairs, mb_pairs = d_sb // LANES, d_mb // LANES

    c_pad = jnp.pad(c, ((0, (-c.shape[0]) % 8), (0, 0)))
    bias_tiles = _moba_bias_tiles(rel_bias)

    for l in range(depth):
        mod = _modulation(c_pad, w_ada[l], b_ada[l][None, :])[:b].reshape(b, 6, d)
        w = w_in[l]
        w_lo = jnp.concatenate([w[:, :3 * d_sb], w[:, 3 * d_sb + 2 * d_mb:]], axis=1).astype(BF16)
        w_hi = w[:, 3 * d_sb:3 * d_sb + 2 * d_mb]
        w_hh = w_hi.astype(BF16)
        w_hl = (w_hi - w_hh.astype(F32)).astype(BF16)
        qkv, qk_mb = _in_projection(x, mod, w_lo, w_hh, w_hl, tm)

        o_sb = _sb_attention(qkv, sb_gain[l][None, :], sb_pairs, 0, sb_pairs, 2 * sb_pairs)
        sel, q_mb, k_mb = _moba_gate(qk_mb, mb_pairs)
        o_mb = _moba_attention(q_mb, k_mb, qkv, sel, bias_tiles, mb_gain[l][None, :], mb_pairs, 3 * sb_pairs)

        wo = w_out[l].astype(BF16)
        x = _out_projection(o_sb, o_mb, x, mod, wo[:d_sb], wo[d_sb:], ln1_g[l][None, :], ln1_b[l][None, :],
                            alpha, tm)
        x = _feed_forward(x, mod, w_gate[l].astype(BF16), w_up[l].astype(BF16), w_down[l].astype(BF16),
                          ln2_g[l][None, :], ln2_b[l][None, :], alpha, tm)
    return x
```

```python
import functools
import math

import jax
import jax.numpy as jnp
from jax import lax
from jax.experimental import pallas as pl
from jax.experimental.pallas import tpu as pltpu

HEAD_DIM = 64
MOBA_BLOCK = 256
MOBA_TOPK = 3
N_BUCKETS = 32
MAX_DISTANCE = 128
LN_EPS = 1e-5
RMS_EPS = 1e-6
NEG_BIG = -1e30
SB_DEAD_LOG = 120.0
LOG2_E = 1.4426950408889634
FF_CHUNK = 1024
MOBA_PAD_ROWS = 16
SB_PARKED = 1e30
LANES = 128
ATT_BLOCK = 256
VMEM_LIMIT = 56 * 1024 * 1024

F32 = jnp.float32
BF16 = jnp.bfloat16


def _cparams(sem):
    return pltpu.CompilerParams(dimension_semantics=sem, vmem_limit_bytes=VMEM_LIMIT)


def _split(a):
    hi = a.astype(BF16)
    lo = (a - hi.astype(F32)).astype(BF16)
    return hi, lo


def _dot(a, b):
    return jnp.dot(a, b, preferred_element_type=F32)


def _dot_nt(a, b):
    return lax.dot_general(a, b, (((1,), (1,)), ((), ())), preferred_element_type=F32)


def _layer_norm(r, g, b):
    mu = jnp.mean(r, axis=-1, keepdims=True)
    d = r - mu
    var = jnp.mean(d * d, axis=-1, keepdims=True)
    return d * lax.rsqrt(var + LN_EPS) * g + b


def _mod_kernel(c_ref, w_ref, b_ref, o_ref):
    c = c_ref[...]
    cond = c * jax.nn.sigmoid(c)
    c_hi, c_lo = _split(cond)
    w_hi, w_lo = _split(w_ref[...])
    o_ref[...] = _dot(c_hi, w_hi) + _dot(c_hi, w_lo) + _dot(c_lo, w_hi) + b_ref[...]


def _modulation(c_pad, w, b):
    rows, d = c_pad.shape
    n = w.shape[1]
    tn = n // 4 if n % (4 * LANES) == 0 else n
    return pl.pallas_call(
        _mod_kernel,
        grid=(n // tn,),
        in_specs=[pl.BlockSpec((rows, d), lambda j: (0, 0)),
                  pl.BlockSpec((d, tn), lambda j: (0, j)),
                  pl.BlockSpec((1, tn), lambda j: (0, j))],
        out_specs=pl.BlockSpec((rows, tn), lambda j: (0, j)),
        out_shape=jax.ShapeDtypeStruct((rows, n), F32),
        compiler_params=_cparams(("arbitrary",)),
        name="adaln_mod",
    )(c_pad, w, b)


def _inproj_kernel(x_ref, mod_ref, wlo_ref, whh_ref, whl_ref, olo_ref, ohi_ref):
    x = x_ref[0]
    shift = mod_ref[0, 0:1, :]
    scale = mod_ref[0, 1:2, :]
    u = x * (1.0 + scale) + shift
    u_hi, u_lo = _split(u)
    olo_ref[0] = _dot(u_hi, wlo_ref[...]).astype(BF16)
    whh = whh_ref[...]
    ohi_ref[0] = _dot(u_hi, whh) + _dot(u_hi, whl_ref[...]) + _dot(u_lo, whh)


def _in_projection(x, mod, w_lo, w_hh, w_hl, tm):
    b, s, d = x.shape
    n_lo, n_hi = w_lo.shape[1], w_hh.shape[1]
    const = lambda bi, i: (0, 0)
    return pl.pallas_call(
        _inproj_kernel,
        grid=(b, s // tm),
        in_specs=[pl.BlockSpec((1, tm, d), lambda bi, i: (bi, i, 0)),
                  pl.BlockSpec((1, 6, d), lambda bi, i: (bi, 0, 0)),
                  pl.BlockSpec((d, n_lo), const),
                  pl.BlockSpec((d, n_hi), const),
                  pl.BlockSpec((d, n_hi), const)],
        out_specs=[pl.BlockSpec((1, tm, n_lo), lambda bi, i: (bi, i, 0)),
                   pl.BlockSpec((1, tm, n_hi), lambda bi, i: (bi, i, 0))],
        out_shape=[jax.ShapeDtypeStruct((b, s, n_lo), BF16),
                   jax.ShapeDtypeStruct((b, s, n_hi), F32)],
        compiler_params=_cparams(("arbitrary", "arbitrary")),
        name="in_proj",
    )(x, mod, w_lo, w_hh, w_hl)


def _store_values_transposed(v_ref, vt_ref):
    t = ATT_BLOCK
    for n in range(vt_ref.shape[0]):
        vt_ref[n] = v_ref[0, n * t:(n + 1) * t, :].astype(F32).T.astype(BF16)


def _head_rms(o0_t, o1_t, gain):
    def norm(o_t):
        ms = jnp.mean(o_t * o_t, axis=0, keepdims=True)
        return o_t * lax.rsqrt(ms + RMS_EPS)
    both = jnp.concatenate([norm(o0_t), norm(o1_t)], axis=0)
    return both.T * gain


def _masked_heads(q2, scale):
    lane = lax.broadcasted_iota(jnp.int32, q2.shape, 1)
    lo = lane < HEAD_DIM
    return [(jnp.where(lo, q2, 0.0) * scale).astype(BF16),
            (jnp.where(lo, 0.0, q2) * scale).astype(BF16)]


def _sb_kernel(q_ref, k_ref, v_ref, g_ref, o_ref, vt_ref, acc_ref, z_ref, w_ref):
    t = ATT_BLOCK
    gi = pl.program_id(2)

    @pl.when(gi == 0)
    def _():
        _store_values_transposed(v_ref, vt_ref)

    qh = []
    for j in range(2):
        qh += _masked_heads(q_ref[0, j * t:(j + 1) * t, :].astype(F32), HEAD_DIM ** -0.5)
    row = lax.broadcasted_iota(jnp.int32, (t, t), 0)
    col = lax.broadcasted_iota(jnp.int32, (t, t), 1)
    past = row < col
    tri = jnp.where(col >= row, 1.0, 0.0).astype(BF16)
    last_row = lax.broadcasted_iota(jnp.int32, (8, t), 0) == 7

    def key_block(r, j):
        return jnp.maximum(2 * gi + j - r, 0)

    def logits(r):
        for j in range(2):
            k2 = k_ref[0, pl.ds(pl.multiple_of(key_block(r, j) * t, t), t), :]
            for h in range(2):
                z_ref[r % 2, 2 * j + h] = _dot_nt(k2, qh[2 * j + h])

    def weights(r, carries, diag):
        out = []
        for s in range(4):
            carry = carries[s] if diag else jnp.where(2 * gi + s // 2 - r >= 0, carries[s], SB_PARKED)
            z = z_ref[r % 2, s]
            neg_abs = pltpu.bitcast(pltpu.bitcast(z, jnp.uint32) | jnp.uint32(0x80000000), F32)
            a = jnp.maximum(z, 0.0) + jnp.log(1.0 + jnp.exp(neg_abs))
            if diag:
                a = jnp.where(past, a, 0.0)
            a = jnp.concatenate([a[:t - 8], a[t - 8:] + jnp.where(last_row, carry, 0.0)], axis=0)
            ainc = _dot(tri, a.astype(BF16))
            w = jnp.exp(z - ainc)
            if diag:
                w = jnp.where(past, w, 0.0)
            w_ref[r % 2, s] = w.astype(BF16)
            out.append(ainc[0:1, :])
        return out

    def values(r):
        for s in range(4):
            h = s % 2
            vt = vt_ref[key_block(r, s // 2), h * HEAD_DIM:(h + 1) * HEAD_DIM, :]
            acc_ref[s] += _dot(vt, w_ref[r % 2, s])

    def any_alive(carries, r_next):
        least = jnp.min(jnp.minimum(jnp.minimum(carries[0], carries[1]), jnp.minimum(carries[2], carries[3])))
        return jnp.logical_and(r_next <= 2 * gi + 1, least < SB_DEAD_LOG).astype(jnp.int32)

    acc_ref[...] = jnp.zeros(acc_ref.shape, F32)
    zero = jnp.zeros((1, t), F32)
    logits(0)
    logits(1)
    carries = weights(0, [zero] * 4, True)

    def body(st):
        r = st[1]
        values(r - 1)
        out = weights(r, st[2:], False)
        logits(r + 1)
        return (any_alive(out, r + 1), r + 1, *out)

    st = lax.while_loop(lambda st: st[0] > 0, body, (any_alive(carries, 1), jnp.int32(1), *carries))
    values(st[1] - 1)
    for j in range(2):
        o_ref[0, j * t:(j + 1) * t, :] = _head_rms(acc_ref[2 * j], acc_ref[2 * j + 1], g_ref[...]).astype(o_ref.dtype)


def _sb_attention(qkv, gain, n_pairs, q_blk0, k_blk0, v_blk0):
    b, s, _ = qkv.shape
    t = ATT_BLOCK
    return pl.pallas_call(
        _sb_kernel,
        grid=(b, n_pairs, s // (2 * t)),
        in_specs=[pl.BlockSpec((1, 2 * t, LANES), lambda bi, p, i: (bi, i, q_blk0 + p)),
                  pl.BlockSpec((1, s, LANES), lambda bi, p, i: (bi, 0, k_blk0 + p)),
                  pl.BlockSpec((1, s, LANES), lambda bi, p, i: (bi, 0, v_blk0 + p)),
                  pl.BlockSpec((1, LANES), lambda bi, p, i: (0, p))],
        out_specs=pl.BlockSpec((1, 2 * t, LANES), lambda bi, p, i: (bi, i, p)),
        out_shape=jax.ShapeDtypeStruct((b, s, n_pairs * LANES), BF16),
        scratch_shapes=[pltpu.VMEM((s // t, LANES, t), BF16),
                        pltpu.VMEM((4, HEAD_DIM, t), F32),
                        pltpu.VMEM((2, 4, t, t), F32),
                        pltpu.VMEM((2, 4, t, t), BF16)],
        compiler_params=_cparams(("arbitrary", "arbitrary", "arbitrary")),
        name="sb_attention",
    )(qkv, qkv, qkv, gain)


def _moba_gate_kernel(q_ref, k_ref, sel_ref, qb_ref, kb_ref):
    t = ATT_BLOCK
    s_len = k_ref.shape[1]
    nb = s_len // t
    kf = k_ref[0]
    q2 = q_ref[0]
    kb_ref[0] = kf.astype(BF16)
    qb_ref[0] = (q2 * (HEAD_DIM ** -0.5 * LOG2_E)).astype(BF16)
    km = jnp.mean(kf.reshape(nb, t, LANES), axis=1)
    km_hi, km_lo = _split(km)
    km_both = jnp.concatenate([km_hi, km_lo], axis=0)
    lane = lax.broadcasted_iota(jnp.int32, (s_len, LANES), 1)
    blk = lax.broadcasted_iota(jnp.int32, (nb, s_len), 0)
    blk_f = blk.astype(F32)
    q_blk = lax.broadcasted_iota(jnp.int32, (nb, s_len), 1) // t
    fully_past = blk < q_blk
    for h in range(2):
        qh = jnp.where(lane < HEAD_DIM, q2, 0.0) if h == 0 else jnp.where(lane < HEAD_DIM, 0.0, q2)
        q_hi, q_lo = _split(qh)
        hh_lh = _dot_nt(km_both, q_hi)
        gate = hh_lh[:nb] + hh_lh[nb:] + _dot_nt(km_hi, q_lo)
        g = jnp.where(fully_past, gate, -jnp.inf)
        mask = jnp.full((nb, s_len), NEG_BIG, F32)
        for _ in range(MOBA_TOPK):
            mx = jnp.max(g, axis=0, keepdims=True)
            first = jnp.min(jnp.where(g == mx, blk_f, float(nb)), axis=0, keepdims=True)
            pick = blk_f == first
            mask = jnp.where(pick, 0.0, mask)
            g = jnp.where(pick, -jnp.inf, g)
        mask = jnp.where(fully_past, mask, NEG_BIG)
        sel_ref[0, 0, h] = jnp.where(blk == q_blk, 0.0, mask)


def _moba_gate(qk, n_pairs):
    b, s, _ = qk.shape
    nb = s // ATT_BLOCK
    return pl.pallas_call(
        _moba_gate_kernel,
        grid=(b, n_pairs),
        in_specs=[pl.BlockSpec((1, s, LANES), lambda bi, p: (bi, 0, p)),
                  pl.BlockSpec((1, s, LANES), lambda bi, p: (bi, 0, n_pairs + p))],
        out_specs=[pl.BlockSpec((1, 1, 2, nb, s), lambda bi, p: (bi, p, 0, 0, 0)),
                   pl.BlockSpec((1, s, LANES), lambda bi, p: (bi, 0, p)),
                   pl.BlockSpec((1, s, LANES), lambda bi, p: (bi, 0, p))],
        out_shape=[jax.ShapeDtypeStruct((b, n_pairs, 2, nb, s), F32),
                   jax.ShapeDtypeStruct((b, s, n_pairs * LANES), BF16),
                   jax.ShapeDtypeStruct((b, s, n_pairs * LANES), BF16)],
        compiler_params=_cparams(("arbitrary", "arbitrary")),
        name="moba_gate",
    )(qk, qk)


def _moba_kernel(q_ref, k_ref, v_ref, sel_ref, bias_ref, g_ref, o_ref, vt_ref, s_ref, acc_ref):
    t = ATT_BLOCK
    gi = pl.program_id(2)

    @pl.when(gi == 0)
    def _():
        ones_row = jnp.where(lax.broadcasted_iota(jnp.int32, (MOBA_PAD_ROWS, t), 0) == 0, 1.0, 0.0).astype(BF16)
        for n in range(vt_ref.shape[0]):
            vt = v_ref[0, n * t:(n + 1) * t, :].astype(F32).T.astype(BF16)
            for h in range(2):
                vt_ref[n, h, :HEAD_DIM] = vt[h * HEAD_DIM:(h + 1) * HEAD_DIM]
                vt_ref[n, h, HEAD_DIM:] = ones_row

    qs = []
    for j in range(2):
        qs += _masked_heads(q_ref[0, j * t:(j + 1) * t, :].astype(F32), 1.0)

    def tile_kind(n, j):
        return jnp.clip(n - (2 * gi + j) + 2, 0, 3)

    def score_blocks(first, count, mx):
        mx = list(mx)
        for n in [first + c for c in range(count)]:
            start = n * t if isinstance(n, int) else pl.multiple_of(n * t, t)
            k2 = k_ref[0, pl.ds(start, t), :]
            for slot in range(4):
                j, h = slot // 2, slot % 2
                s = _dot_nt(k2, qs[slot]) + bias_ref[h, tile_kind(n, j)]
                s = s + sel_ref[0, 0, h, pl.ds(n, 1), j * t:(j + 1) * t]
                s_ref[slot, n] = s
                mx[slot] = jnp.maximum(mx[slot], jnp.max(s, axis=0, keepdims=True))
        return tuple(mx)

    def weigh_blocks(first, count, mx):
        blocks = [first + c for c in range(count)]
        for slot in range(4):
            p = [jnp.exp2((s_ref[slot, n] - mx[slot]).astype(BF16)) for n in blocks]
            vt = [vt_ref[n, slot % 2] for n in blocks]
            acc_ref[slot] += _dot(jnp.concatenate(vt, axis=1), jnp.concatenate(p, axis=0))

    n_blocks = 2 * gi + 2
    lead = n_blocks % 4
    mx = (jnp.full((1, t), -jnp.inf, F32),) * 4
    mx = lax.cond(lead > 0, lambda m: score_blocks(0, 2, m), lambda m: m, mx)
    mx = lax.fori_loop(0, n_blocks // 4, lambda i, m: score_blocks(lead + 4 * i, 4, m), mx)

    acc_ref[...] = jnp.zeros(acc_ref.shape, F32)

    @pl.when(lead > 0)
    def _():
        weigh_blocks(0, 2, mx)

    def weigh_trip(i, c):
        weigh_blocks(lead + 4 * i, 4, mx)
        return c

    lax.fori_loop(0, n_blocks // 4, weigh_trip, 0)
    for j in range(2):
        o = [acc_ref[2 * j + h, :HEAD_DIM] / acc_ref[2 * j + h, HEAD_DIM:HEAD_DIM + 1] for h in range(2)]
        o_ref[0, j * t:(j + 1) * t, :] = _head_rms(o[0], o[1], g_ref[...]).astype(o_ref.dtype)


def _moba_attention(qb, kb, qkv, sel, bias, gain, n_pairs, v_blk0):
    b, s, _ = qb.shape
    t = ATT_BLOCK
    nb = s // t
    return pl.pallas_call(
        _moba_kernel,
        grid=(b, n_pairs, nb // 2),
        in_specs=[pl.BlockSpec((1, 2 * t, LANES), lambda bi, p, i: (bi, i, p)),
                  pl.BlockSpec((1, s, LANES), lambda bi, p, i: (bi, 0, p)),
                  pl.BlockSpec((1, s, LANES), lambda bi, p, i: (bi, 0, v_blk0 + p)),
                  pl.BlockSpec((1, 1, 2, nb, 2 * t), lambda bi, p, i: (bi, p, 0, 0, i)),
                  pl.BlockSpec((2, 4, t, t), lambda bi, p, i: (p, 0, 0, 0)),
                  pl.BlockSpec((1, LANES), lambda bi, p, i: (0, p))],
        out_specs=pl.BlockSpec((1, 2 * t, LANES), lambda bi, p, i: (bi, i, p)),
        out_shape=jax.ShapeDtypeStruct((b, s, n_pairs * LANES), BF16),
        scratch_shapes=[pltpu.VMEM((nb, 2, HEAD_DIM + MOBA_PAD_ROWS, t), BF16),
                        pltpu.VMEM((4, nb, t, t), F32),
                        pltpu.VMEM((4, HEAD_DIM + MOBA_PAD_ROWS, t), F32)],
        compiler_params=_cparams(("arbitrary", "arbitrary", "arbitrary")),
        name="moba_attention",
    )(qb, kb, qkv, sel, bias, gain)


def _mlp_kernel(osb_ref, omb_ref, x_ref, mod_ref, w1_ref, w2_ref, g1_ref, b1_ref,
                wg_ref, wu_ref, wd_ref, g2_ref, b2_ref, o_ref, *, alpha, chunks):
    y = _dot(osb_ref[0], w1_ref[...]) + _dot(omb_ref[0], w2_ref[...])
    x1 = _layer_norm(alpha * x_ref[0] + mod_ref[0, 2:3, :] * y, g1_ref[...], b1_ref[...])
    u = (x1 * (1.0 + mod_ref[0, 4:5, :]) + mod_ref[0, 3:4, :]).astype(BF16)
    f = None
    for c0, c1 in chunks:
        hg = _dot(u, wg_ref[:, c0:c1])
        hu = _dot(u, wu_ref[:, c0:c1])
        hidden = (hg * jax.nn.sigmoid(hg) * hu).astype(BF16)
        part = _dot(hidden, wd_ref[c0:c1, :])
        f = part if f is None else f + part
    o_ref[0] = _layer_norm(alpha * x1 + mod_ref[0, 5:6, :] * f, g2_ref[...], b2_ref[...])


def _hidden_chunks(d_ff):
    edges = list(range(0, d_ff, FF_CHUNK)) + [d_ff]
    return tuple(zip(edges[:-1], edges[1:]))


def _mlp(o_sb, o_mb, x, mod, w1, w2, ln1_g, ln1_b, wg, wu, wd, ln2_g, ln2_b, alpha, tm):
    b, s, d = x.shape
    h1, h2 = w1.shape[0], w2.shape[0]
    d_ff = wg.shape[1]
    const = lambda bi, i: (0, 0)
    resident = lambda shape: pl.BlockSpec(shape, const, pipeline_mode=pl.Buffered(1))
    return pl.pallas_call(
        functools.partial(_mlp_kernel, alpha=alpha, chunks=_hidden_chunks(d_ff)),
        grid=(b, s // tm),
        in_specs=[pl.BlockSpec((1, tm, h1), lambda bi, i: (bi, i, 0)),
                  pl.BlockSpec((1, tm, h2), lambda bi, i: (bi, i, 0)),
                  pl.BlockSpec((1, tm, d), lambda bi, i: (bi, i, 0)),
                  pl.BlockSpec((1, 6, d), lambda bi, i: (bi, 0, 0)),
                  resident((h1, d)), resident((h2, d)), resident((1, d)), resident((1, d)),
                  resident((d, d_ff)), resident((d, d_ff)), resident((d_ff, d)),
                  resident((1, d)), resident((1, d))],
        out_specs=pl.BlockSpec((1, tm, d), lambda bi, i: (bi, i, 0)),
        out_shape=jax.ShapeDtypeStruct((b, s, d), F32),
        compiler_params=_cparams(("arbitrary", "arbitrary")),
        name="out_proj_ffn",
    )(o_sb, o_mb, x, mod, w1, w2, ln1_g, ln1_b, wg, wu, wd, ln2_g, ln2_b)


def _t5_bucket(dist):
    n = jnp.maximum(dist, 0)
    max_exact = N_BUCKETS // 2
    nf = jnp.maximum(n, 1).astype(F32)
    large = max_exact + (jnp.log(nf / max_exact) / math.log(MAX_DISTANCE / max_exact)
                         * (N_BUCKETS - max_exact)).astype(jnp.int32)
    large = jnp.minimum(large, N_BUCKETS - 1)
    return jnp.where(n < max_exact, n, large)


def _bucket_lookup(rel_h, bucket):
    out = jnp.zeros((rel_h.shape[0],) + bucket.shape, F32)
    for b in range(N_BUCKETS):
        out = jnp.where(bucket[None] == b, rel_h[:, b].reshape((-1,) + (1,) * bucket.ndim), out)
    return out


def _moba_bias_tiles(rel_bias):
    t = ATT_BLOCK
    j = jnp.arange(t)[:, None]
    i = jnp.arange(t)[None, :]
    rel_h = rel_bias.T.astype(F32) * LOG2_E
    far = _bucket_lookup(rel_h, _t5_bucket(jnp.full((t, t), 2 * t)))
    prev = _bucket_lookup(rel_h, _t5_bucket(t + i - j))
    own = jnp.where((j <= i)[None], _bucket_lookup(rel_h, _t5_bucket(i - j)), NEG_BIG)
    return jnp.stack([far, prev, own, jnp.full_like(far, NEG_BIG)], axis=1)


def kernel(x, c, w_ada, b_ada, w_in, sb_gain, mb_gain, w_out, ln1_g, ln1_b,
           w_gate, w_up, w_down, ln2_g, ln2_b, rel_bias):
    depth = w_ada.shape[0]
    b, s, d = x.shape
    d_sb = sb_gain.shape[1]
    d_mb = mb_gain.shape[1]
    assert ATT_BLOCK == MOBA_BLOCK and MOBA_BLOCK >= 2 * MAX_DISTANCE
    assert s % ATT_BLOCK == 0 and (s // ATT_BLOCK) % 8 == 0
    assert d_sb % LANES == 0 and d_mb % LANES == 0 and d % LANES == 0
    alpha = float((2 * depth) ** 0.25)
    tm = min(512, s)
    sb_pairs, mb_pairs = d_sb // LANES, d_mb // LANES

    c_pad = jnp.pad(c, ((0, (-c.shape[0]) % 8), (0, 0)))
    bias_tiles = _moba_bias_tiles(rel_bias)

    for l in range(depth):
        mod = _modulation(c_pad, w_ada[l], b_ada[l][None, :])[:b].reshape(b, 6, d)
        w = w_in[l]
        w_lo = jnp.concatenate([w[:, :3 * d_sb], w[:, 3 * d_sb + 2 * d_mb:]], axis=1).astype(BF16)
        w_hi = w[:, 3 * d_sb:3 * d_sb + 2 * d_mb]
        w_hh = w_hi.astype(BF16)
        w_hl = (w_hi - w_hh.astype(F32)).astype(BF16)
        qkv, qk_mb = _in_projection(x, mod, w_lo, w_hh, w_hl, tm)

        o_sb = _sb_attention(qkv, sb_gain[l][None, :], sb_pairs, 0, sb_pairs, 2 * sb_pairs)
        sel, q_mb, k_mb = _moba_gate(qk_mb, mb_pairs)
        o_mb = _moba_attention(q_mb, k_mb, qkv, sel, bias_tiles, mb_gain[l][None, :], mb_pairs, 3 * sb_pairs)

        wo = w_out[l].astype(BF16)
        x = _mlp(o_sb, o_mb, x, mod, wo[:d_sb], wo[d_sb:], ln1_g[l][None, :], ln1_b[l][None, :],
                 w_gate[l].astype(BF16), w_up[l].astype(BF16), w_down[l].astype(BF16),
                 ln2_g[l][None, :], ln2_b[l][None, :], alpha, tm)
    return x
```

```python
import functools
import math

import jax
import jax.numpy as jnp
import numpy as np
from jax import lax
from jax.experimental import pallas as pl
from jax.experimental.pallas import tpu as pltpu

HEAD_DIM = 64
MOBA_BLOCK = 256
MOBA_TOPK = 3
N_BUCKETS = 32
MAX_DISTANCE = 128
LN_EPS = 1e-5
RMS_EPS = 1e-6
NEG_BIG = -1e30
SB_DEAD_LOG = 120.0
LOG2_E = 1.4426950408889634
FF_CHUNK = 1024
MOBA_PAD_ROWS = 16
SB_PARKED = 1e30
LANES = 128
ATT_BLOCK = 256
VMEM_LIMIT = 56 * 1024 * 1024

F32 = jnp.float32
BF16 = jnp.bfloat16


def _cparams(sem):
    return pltpu.CompilerParams(dimension_semantics=sem, vmem_limit_bytes=VMEM_LIMIT)


def _split(a):
    hi = a.astype(BF16)
    lo = (a - hi.astype(F32)).astype(BF16)
    return hi, lo


def _dot(a, b):
    return jnp.dot(a, b, preferred_element_type=F32)


def _dot_nt(a, b):
    return lax.dot_general(a, b, (((1,), (1,)), ((), ())), preferred_element_type=F32)


def _layer_norm(r, g, b):
    mu = jnp.mean(r, axis=-1, keepdims=True)
    d = r - mu
    var = jnp.mean(d * d, axis=-1, keepdims=True)
    return d * lax.rsqrt(var + LN_EPS) * g + b


def _mod_kernel(c_ref, w_ref, b_ref, o_ref):
    c = c_ref[...]
    cond = c * jax.nn.sigmoid(c)
    c_hi, c_lo = _split(cond)
    w_hi, w_lo = _split(w_ref[...])
    o_ref[...] = _dot(c_hi, w_hi) + _dot(c_hi, w_lo) + _dot(c_lo, w_hi) + b_ref[...]


def _modulation(c_pad, w, b):
    rows, d = c_pad.shape
    n = w.shape[1]
    tn = n // 4 if n % (4 * LANES) == 0 else n
    return pl.pallas_call(
        _mod_kernel,
        grid=(n // tn,),
        in_specs=[pl.BlockSpec((rows, d), lambda j: (0, 0)),
                  pl.BlockSpec((d, tn), lambda j: (0, j)),
                  pl.BlockSpec((1, tn), lambda j: (0, j))],
        out_specs=pl.BlockSpec((rows, tn), lambda j: (0, j)),
        out_shape=jax.ShapeDtypeStruct((rows, n), F32),
        compiler_params=_cparams(("arbitrary",)),
        name="adaln_mod",
    )(c_pad, w, b)


def _inproj_kernel(x_ref, mod_ref, w_ref, o_ref):
    u = x_ref[0] * (1.0 + mod_ref[0, 1:2, :]) + mod_ref[0, 0:1, :]
    o_ref[0] = _dot(u.astype(BF16), w_ref[...]).astype(BF16)


def _in_projection(x, mod, w, tm):
    b, s, d = x.shape
    n = w.shape[1]
    return pl.pallas_call(
        _inproj_kernel,
        grid=(b, s // tm),
        in_specs=[pl.BlockSpec((1, tm, d), lambda bi, i: (bi, i, 0)),
                  pl.BlockSpec((1, 6, d), lambda bi, i: (bi, 0, 0)),
                  pl.BlockSpec((d, n), lambda bi, i: (0, 0), pipeline_mode=pl.Buffered(1))],
        out_specs=pl.BlockSpec((1, tm, n), lambda bi, i: (bi, i, 0)),
        out_shape=jax.ShapeDtypeStruct((b, s, n), BF16),
        compiler_params=_cparams(("arbitrary", "arbitrary")),
        name="in_proj",
    )(x, mod, w)


def _store_values_transposed(v_ref, vt_ref):
    t = ATT_BLOCK
    for n in range(vt_ref.shape[0]):
        vt_ref[n] = v_ref[0, n * t:(n + 1) * t, :].astype(F32).T.astype(BF16)


def _head_rms(o0_t, o1_t, gain):
    def norm(o_t):
        ms = jnp.mean(o_t * o_t, axis=0, keepdims=True)
        return o_t * lax.rsqrt(ms + RMS_EPS)
    both = jnp.concatenate([norm(o0_t), norm(o1_t)], axis=0)
    return both.T * gain


def _masked_heads(q2):
    q2 = q2.astype(F32)
    lo = lax.broadcasted_iota(jnp.int32, q2.shape, 1) < HEAD_DIM
    return [jnp.where(lo, q2, 0.0).astype(BF16), jnp.where(lo, 0.0, q2).astype(BF16)]


def _sb_kernel(q_ref, k_ref, v_ref, g_ref, o_ref, vt_ref, acc_ref, z_ref, w_ref):
    t = ATT_BLOCK
    gi = pl.program_id(2)

    @pl.when(gi == 0)
    def _():
        _store_values_transposed(v_ref, vt_ref)

    qh = []
    for j in range(2):
        qh += _masked_heads(q_ref[0, j * t:(j + 1) * t, :])
    row = lax.broadcasted_iota(jnp.int32, (t, t), 0)
    col = lax.broadcasted_iota(jnp.int32, (t, t), 1)
    past = row < col
    tri = jnp.where(col >= row, 1.0, 0.0).astype(BF16)
    last_row = lax.broadcasted_iota(jnp.int32, (8, t), 0) == 7

    def key_block(r, j):
        return jnp.maximum(2 * gi + j - r, 0)

    def logits(r):
        for j in range(2):
            k2 = k_ref[0, pl.ds(pl.multiple_of(key_block(r, j) * t, t), t), :]
            for h in range(2):
                z_ref[r % 2, 2 * j + h] = _dot_nt(k2, qh[2 * j + h])

    def weights(r, carries, diag):
        out = []
        for s in range(4):
            carry = carries[s] if diag else jnp.where(2 * gi + s // 2 - r >= 0, carries[s], SB_PARKED)
            z = z_ref[r % 2, s]
            a = jnp.maximum(z, 0.0) + jnp.log(1.0 + jnp.exp(-jnp.abs(z)))
            if diag:
                a = jnp.where(past, a, 0.0)
            a = jnp.concatenate([a[:t - 8], a[t - 8:] + jnp.where(last_row, carry, 0.0)], axis=0)
            ainc = _dot(tri, a.astype(BF16))
            w = jnp.exp(z - ainc)
            if diag:
                w = jnp.where(past, w, 0.0)
            w_ref[r % 2, s] = w.astype(BF16)
            out.append(ainc[0:1, :])
        return out

    def values(r):
        for s in range(4):
            h = s % 2
            vt = vt_ref[key_block(r, s // 2), h * HEAD_DIM:(h + 1) * HEAD_DIM, :]
            acc_ref[s] += _dot(vt, w_ref[r % 2, s])

    def any_alive(carries, r_next):
        least = jnp.min(jnp.minimum(jnp.minimum(carries[0], carries[1]), jnp.minimum(carries[2], carries[3])))
        return jnp.logical_and(r_next <= 2 * gi + 1, least < SB_DEAD_LOG).astype(jnp.int32)

    acc_ref[...] = jnp.zeros(acc_ref.shape, F32)
    zero = jnp.zeros((1, t), F32)
    logits(0)
    logits(1)
    carries = weights(0, [zero] * 4, True)

    def body(st):
        r = st[1]
        values(r - 1)
        out = weights(r, st[2:], False)
        logits(r + 1)
        return (any_alive(out, r + 1), r + 1, *out)

    st = lax.while_loop(lambda st: st[0] > 0, body, (any_alive(carries, 1), jnp.int32(1), *carries))
    values(st[1] - 1)
    for j in range(2):
        o_ref[0, j * t:(j + 1) * t, :] = _head_rms(acc_ref[2 * j], acc_ref[2 * j + 1], g_ref[...]).astype(o_ref.dtype)


def _sb_attention(qkv, gain, n_pairs, q_blk0, k_blk0, v_blk0):
    b, s, _ = qkv.shape
    t = ATT_BLOCK
    return pl.pallas_call(
        _sb_kernel,
        grid=(b, n_pairs, s // (2 * t)),
        in_specs=[pl.BlockSpec((1, 2 * t, LANES), lambda bi, p, i: (bi, i, q_blk0 + p)),
                  pl.BlockSpec((1, s, LANES), lambda bi, p, i: (bi, 0, k_blk0 + p)),
                  pl.BlockSpec((1, s, LANES), lambda bi, p, i: (bi, 0, v_blk0 + p)),
                  pl.BlockSpec((1, LANES), lambda bi, p, i: (0, p))],
        out_specs=pl.BlockSpec((1, 2 * t, LANES), lambda bi, p, i: (bi, i, p)),
        out_shape=jax.ShapeDtypeStruct((b, s, n_pairs * LANES), BF16),
        scratch_shapes=[pltpu.VMEM((s // t, LANES, t), BF16),
                        pltpu.VMEM((4, HEAD_DIM, t), F32),
                        pltpu.VMEM((2, 4, t, t), F32),
                        pltpu.VMEM((2, 4, t, t), BF16)],
        compiler_params=_cparams(("arbitrary", "arbitrary", "arbitrary")),
        name="sb_attention",
    )(qkv, qkv, qkv, gain)


def _moba_gate_kernel(q_ref, k_ref, sel_ref):
    t = ATT_BLOCK
    s_len = k_ref.shape[1]
    nb = s_len // t
    q2 = q_ref[0]
    km = jnp.mean(k_ref[0].astype(F32).reshape(nb, t, LANES), axis=1)
    lane = lax.broadcasted_iota(jnp.int32, (nb, LANES), 1)
    blk = lax.broadcasted_iota(jnp.int32, (nb, s_len), 0)
    blk_f = blk.astype(F32)
    q_blk = lax.broadcasted_iota(jnp.int32, (nb, s_len), 1) // t
    fully_past = blk < q_blk
    for h in range(2):
        in_head = (lane < HEAD_DIM) if h == 0 else (lane >= HEAD_DIM)
        km_hi, km_lo = _split(jnp.where(in_head, km, 0.0))
        hi_lo = _dot_nt(jnp.concatenate([km_hi, km_lo], axis=0), q2)
        gate = hi_lo[:nb] + hi_lo[nb:]
        g = jnp.where(fully_past, gate, -jnp.inf)
        mask = jnp.full((nb, s_len), NEG_BIG, F32)
        for _ in range(MOBA_TOPK):
            mx = jnp.max(g, axis=0, keepdims=True)
            first = jnp.min(jnp.where(g == mx, blk_f, float(nb)), axis=0, keepdims=True)
            pick = blk_f == first
            mask = jnp.where(pick, 0.0, mask)
            g = jnp.where(pick, -jnp.inf, g)
        mask = jnp.where(fully_past, mask, NEG_BIG)
        sel_ref[0, 0, h] = jnp.where(blk == q_blk, 0.0, mask)


def _moba_gate(qkv, n_pairs, q_blk0, k_blk0):
    b, s, _ = qkv.shape
    nb = s // ATT_BLOCK
    return pl.pallas_call(
        _moba_gate_kernel,
        grid=(b, n_pairs),
        in_specs=[pl.BlockSpec((1, s, LANES), lambda bi, p: (bi, 0, q_blk0 + p)),
                  pl.BlockSpec((1, s, LANES), lambda bi, p: (bi, 0, k_blk0 + p))],
        out_specs=pl.BlockSpec((1, 1, 2, nb, s), lambda bi, p: (bi, p, 0, 0, 0)),
        out_shape=jax.ShapeDtypeStruct((b, n_pairs, 2, nb, s), F32),
        compiler_params=_cparams(("arbitrary", "arbitrary")),
        name="moba_gate",
    )(qkv, qkv)


def _moba_kernel(q_ref, k_ref, v_ref, sel_ref, bias_ref, g_ref, o_ref, vt_ref, s_ref, acc_ref):
    t = ATT_BLOCK
    gi = pl.program_id(2)

    @pl.when(gi == 0)
    def _():
        ones_row = jnp.where(lax.broadcasted_iota(jnp.int32, (MOBA_PAD_ROWS, t), 0) == 0, 1.0, 0.0).astype(BF16)
        for n in range(vt_ref.shape[0]):
            vt = v_ref[0, n * t:(n + 1) * t, :].astype(F32).T.astype(BF16)
            for h in range(2):
                vt_ref[n, h, :HEAD_DIM] = vt[h * HEAD_DIM:(h + 1) * HEAD_DIM]
                vt_ref[n, h, HEAD_DIM:] = ones_row

    qs = []
    for j in range(2):
        qs += _masked_heads(q_ref[0, j * t:(j + 1) * t, :])

    def tile_kind(n, j):
        return jnp.clip(n - (2 * gi + j) + 2, 0, 3)

    def score_blocks(first, count, mx):
        mx = list(mx)
        start = first * t if isinstance(first, int) else pl.multiple_of(first * t, t)
        keys = k_ref[0, pl.ds(start, count * t), :]
        for slot in range(4):
            j, h = slot // 2, slot % 2
            qk = _dot_nt(keys, qs[slot])
            for c in range(count):
                n = first + c
                s = qk[c * t:(c + 1) * t] + bias_ref[h, tile_kind(n, j)]
                s = s + sel_ref[0, 0, h, pl.ds(n, 1), j * t:(j + 1) * t]
                s_ref[slot, n] = s
                mx[slot] = jnp.maximum(mx[slot], jnp.max(s, axis=0, keepdims=True))
        return tuple(mx)

    def weigh_blocks(first, count, mx):
        blocks = [first + c for c in range(count)]
        for slot in range(4):
            p = [jnp.exp2((s_ref[slot, n] - mx[slot]).astype(BF16)) for n in blocks]
            vt = [vt_ref[n, slot % 2] for n in blocks]
            acc_ref[slot] += _dot(jnp.concatenate(vt, axis=1), jnp.concatenate(p, axis=0))

    n_blocks = 2 * gi + 2
    lead = n_blocks % 4
    mx = (jnp.full((1, t), -jnp.inf, F32),) * 4
    mx = lax.cond(lead > 0, lambda m: score_blocks(0, 2, m), lambda m: m, mx)
    mx = lax.fori_loop(0, n_blocks // 4, lambda i, m: score_blocks(lead + 4 * i, 4, m), mx)

    acc_ref[...] = jnp.zeros(acc_ref.shape, F32)

    @pl.when(lead > 0)
    def _():
        weigh_blocks(0, 2, mx)

    def weigh_trip(i, c):
        weigh_blocks(lead + 4 * i, 4, mx)
        return c

    lax.fori_loop(0, n_blocks // 4, weigh_trip, 0)
    for j in range(2):
        o = [acc_ref[2 * j + h, :HEAD_DIM] / acc_ref[2 * j + h, HEAD_DIM:HEAD_DIM + 1] for h in range(2)]
        o_ref[0, j * t:(j + 1) * t, :] = _head_rms(o[0], o[1], g_ref[...]).astype(o_ref.dtype)


def _moba_attention(qkv, sel, bias, gain, n_pairs, q_blk0, k_blk0, v_blk0):
    b, s, _ = qkv.shape
    t = ATT_BLOCK
    nb = s // t
    return pl.pallas_call(
        _moba_kernel,
        grid=(b, n_pairs, nb // 2),
        in_specs=[pl.BlockSpec((1, 2 * t, LANES), lambda bi, p, i: (bi, i, q_blk0 + p)),
                  pl.BlockSpec((1, s, LANES), lambda bi, p, i: (bi, 0, k_blk0 + p)),
                  pl.BlockSpec((1, s, LANES), lambda bi, p, i: (bi, 0, v_blk0 + p)),
                  pl.BlockSpec((1, 1, 2, nb, 2 * t), lambda bi, p, i: (bi, p, 0, 0, i)),
                  pl.BlockSpec((2, 4, t, t), lambda bi, p, i: (p, 0, 0, 0)),
                  pl.BlockSpec((1, LANES), lambda bi, p, i: (0, p))],
        out_specs=pl.BlockSpec((1, 2 * t, LANES), lambda bi, p, i: (bi, i, p)),
        out_shape=jax.ShapeDtypeStruct((b, s, n_pairs * LANES), BF16),
        scratch_shapes=[pltpu.VMEM((nb, 2, HEAD_DIM + MOBA_PAD_ROWS, t), BF16),
                        pltpu.VMEM((4, nb, t, t), F32),
                        pltpu.VMEM((4, HEAD_DIM + MOBA_PAD_ROWS, t), F32)],
        compiler_params=_cparams(("arbitrary", "arbitrary", "arbitrary")),
        name="moba_attention",
    )(qkv, qkv, qkv, sel, bias, gain)


def _mlp_kernel(osb_ref, omb_ref, x_ref, mod_ref, w1_ref, w2_ref, g1_ref, b1_ref,
                wg_ref, wu_ref, wd_ref, g2_ref, b2_ref, o_ref, *, alpha, chunks):
    y = _dot(osb_ref[0], w1_ref[...]) + _dot(omb_ref[0], w2_ref[...])
    x1 = _layer_norm(alpha * x_ref[0] + mod_ref[0, 2:3, :] * y, g1_ref[...], b1_ref[...])
    u = (x1 * (1.0 + mod_ref[0, 4:5, :]) + mod_ref[0, 3:4, :]).astype(BF16)
    f = None
    for c0, c1 in chunks:
        hg = _dot(u, wg_ref[:, c0:c1])
        hu = _dot(u, wu_ref[:, c0:c1])
        hidden = (hg * jax.nn.sigmoid(hg) * hu).astype(BF16)
        part = _dot(hidden, wd_ref[c0:c1, :])
        f = part if f is None else f + part
    o_ref[0] = _layer_norm(alpha * x1 + mod_ref[0, 5:6, :] * f, g2_ref[...], b2_ref[...])


def _hidden_chunks(d_ff):
    edges = list(range(0, d_ff, FF_CHUNK)) + [d_ff]
    return tuple(zip(edges[:-1], edges[1:]))


def _mlp(o_sb, o_mb, x, mod, w1, w2, ln1_g, ln1_b, wg, wu, wd, ln2_g, ln2_b, alpha, tm):
    b, s, d = x.shape
    h1, h2 = w1.shape[0], w2.shape[0]
    d_ff = wg.shape[1]
    const = lambda bi, i: (0, 0)
    resident = lambda shape: pl.BlockSpec(shape, const, pipeline_mode=pl.Buffered(1))
    return pl.pallas_call(
        functools.partial(_mlp_kernel, alpha=alpha, chunks=_hidden_chunks(d_ff)),
        grid=(b, s // tm),
        in_specs=[pl.BlockSpec((1, tm, h1), lambda bi, i: (bi, i, 0)),
                  pl.BlockSpec((1, tm, h2), lambda bi, i: (bi, i, 0)),
                  pl.BlockSpec((1, tm, d), lambda bi, i: (bi, i, 0)),
                  pl.BlockSpec((1, 6, d), lambda bi, i: (bi, 0, 0)),
                  resident((h1, d)), resident((h2, d)), resident((1, d)), resident((1, d)),
                  resident((d, d_ff)), resident((d, d_ff)), resident((d_ff, d)),
                  resident((1, d)), resident((1, d))],
        out_specs=pl.BlockSpec((1, tm, d), lambda bi, i: (bi, i, 0)),
        out_shape=jax.ShapeDtypeStruct((b, s, d), F32),
        compiler_params=_cparams(("arbitrary", "arbitrary")),
        name="out_proj_ffn",
    )(o_sb, o_mb, x, mod, w1, w2, ln1_g, ln1_b, wg, wu, wd, ln2_g, ln2_b)


def _t5_bucket(dist):
    n = np.maximum(dist, 0)
    max_exact = N_BUCKETS // 2
    nf = np.maximum(n, 1).astype(np.float32)
    large = max_exact + (np.log(nf / np.float32(max_exact)) / np.float32(math.log(MAX_DISTANCE / max_exact))
                         * np.float32(N_BUCKETS - max_exact)).astype(np.int32)
    large = np.minimum(large, N_BUCKETS - 1)
    return np.where(n < max_exact, n, large)


def _bucket_lookup(rel_h, bucket):
    out = jnp.zeros((rel_h.shape[0],) + bucket.shape, F32)
    for b in range(N_BUCKETS):
        out = jnp.where((bucket == b)[None], rel_h[:, b].reshape((-1,) + (1,) * bucket.ndim), out)
    return out


def _moba_bias_tiles(rel_bias):
    t = ATT_BLOCK
    j = np.arange(t)[:, None]
    i = np.arange(t)[None, :]
    rel_h = rel_bias.T.astype(F32) * LOG2_E
    far = _bucket_lookup(rel_h, _t5_bucket(np.full((t, t), 2 * t)))
    prev = _bucket_lookup(rel_h, _t5_bucket(t + i - j))
    own = jnp.where((j <= i)[None], _bucket_lookup(rel_h, _t5_bucket(i - j)), NEG_BIG)
    return jnp.stack([far, prev, own, jnp.full_like(far, NEG_BIG)], axis=1)


def kernel(x, c, w_ada, b_ada, w_in, sb_gain, mb_gain, w_out, ln1_g, ln1_b,
           w_gate, w_up, w_down, ln2_g, ln2_b, rel_bias):
    depth = w_ada.shape[0]
    b, s, d = x.shape
    d_sb = sb_gain.shape[1]
    d_mb = mb_gain.shape[1]
    assert ATT_BLOCK == MOBA_BLOCK and MOBA_BLOCK >= 2 * MAX_DISTANCE
    assert s % ATT_BLOCK == 0 and (s // ATT_BLOCK) % 8 == 0
    assert d_sb % LANES == 0 and d_mb % LANES == 0 and d % LANES == 0
    alpha = float((2 * depth) ** 0.25)
    tm = min(512, s)
    sb_pairs, mb_pairs = d_sb // LANES, d_mb // LANES

    c_pad = jnp.pad(c, ((0, (-c.shape[0]) % 8), (0, 0)))
    bias_tiles = _moba_bias_tiles(rel_bias)
    col_scale = jnp.concatenate([jnp.full((d_sb,), HEAD_DIM ** -0.5, F32), jnp.ones((2 * d_sb,), F32),
                                 jnp.full((d_mb,), HEAD_DIM ** -0.5 * LOG2_E, F32), jnp.ones((2 * d_mb,), F32)])
    mb0 = 3 * sb_pairs

    for l in range(depth):
        mod = _modulation(c_pad, w_ada[l], b_ada[l][None, :])[:b].reshape(b, 6, d)
        qkv = _in_projection(x, mod, (w_in[l] * col_scale[None, :]).astype(BF16), tm)

        o_sb = _sb_attention(qkv, sb_gain[l][None, :], sb_pairs, 0, sb_pairs, 2 * sb_pairs)
        sel = _moba_gate(qkv, mb_pairs, mb0, mb0 + mb_pairs)
        o_mb = _moba_attention(qkv, sel, bias_tiles, mb_gain[l][None, :], mb_pairs,
                               mb0, mb0 + mb_pairs, mb0 + 2 * mb_pairs)

        wo = w_out[l].astype(BF16)
        x = _mlp(o_sb, o_mb, x, mod, wo[:d_sb], wo[d_sb:], ln1_g[l][None, :], ln1_b[l][None, :],
                 w_gate[l].astype(BF16), w_up[l].astype(BF16), w_down[l].astype(BF16),
                 ln2_g[l][None, :], ln2_b[l][None, :], alpha, tm)
    return x
```

```python
import functools
import math

import jax
import jax.numpy as jnp
import numpy as np
from jax import lax
from jax.experimental import pallas as pl
from jax.experimental.pallas import tpu as pltpu

HEAD_DIM = 64
MOBA_BLOCK = 256
MOBA_TOPK = 3
N_BUCKETS = 32
MAX_DISTANCE = 128
LN_EPS = 1e-5
RMS_EPS = 1e-6
NEG_BIG = -1e30
SB_DEAD_LOG = 120.0
LOG2_E = 1.4426950408889634
FF_CHUNK = 1024
MOBA_PAD_ROWS = 16
SB_PARKED = 1e30
LANES = 128
ATT_BLOCK = 256
VMEM_LIMIT = 56 * 1024 * 1024

F32 = jnp.float32
BF16 = jnp.bfloat16


def _cparams(sem):
    return pltpu.CompilerParams(dimension_semantics=sem, vmem_limit_bytes=VMEM_LIMIT)


def _split(a):
    hi = a.astype(BF16)
    lo = (a - hi.astype(F32)).astype(BF16)
    return hi, lo


def _dot(a, b):
    return jnp.dot(a, b, preferred_element_type=F32)


def _dot_nt(a, b):
    return lax.dot_general(a, b, (((1,), (1,)), ((), ())), preferred_element_type=F32)


def _layer_norm(r, g, b):
    mu = jnp.mean(r, axis=-1, keepdims=True)
    d = r - mu
    var = jnp.mean(d * d, axis=-1, keepdims=True)
    return d * lax.rsqrt(var + LN_EPS) * g + b


def _mod_kernel(c_ref, w_ref, b_ref, o_ref):
    c = c_ref[...]
    cond = c * jax.nn.sigmoid(c)
    c_hi, c_lo = _split(cond)
    w_hi, w_lo = _split(w_ref[...])
    o_ref[...] = _dot(c_hi, w_hi) + _dot(c_hi, w_lo) + _dot(c_lo, w_hi) + b_ref[...]


def _modulation(c_pad, w, b):
    rows, d = c_pad.shape
    n = w.shape[1]
    tn = n // 4 if n % (4 * LANES) == 0 else n
    return pl.pallas_call(
        _mod_kernel,
        grid=(n // tn,),
        in_specs=[pl.BlockSpec((rows, d), lambda j: (0, 0)),
                  pl.BlockSpec((d, tn), lambda j: (0, j)),
                  pl.BlockSpec((1, tn), lambda j: (0, j))],
        out_specs=pl.BlockSpec((rows, tn), lambda j: (0, j)),
        out_shape=jax.ShapeDtypeStruct((rows, n), F32),
        compiler_params=_cparams(("arbitrary",)),
        name="adaln_mod",
    )(c_pad, w, b)


def _inproj_kernel(x_ref, mod_ref, w_ref, o_ref):
    u = x_ref[0] * (1.0 + mod_ref[0, 1:2, :]) + mod_ref[0, 0:1, :]
    o_ref[0] = _dot(u.astype(BF16), w_ref[...]).astype(BF16)


def _in_projection(x, mod, w, tm):
    b, s, d = x.shape
    n = w.shape[1]
    return pl.pallas_call(
        _inproj_kernel,
        grid=(b, s // tm),
        in_specs=[pl.BlockSpec((1, tm, d), lambda bi, i: (bi, i, 0)),
                  pl.BlockSpec((1, 6, d), lambda bi, i: (bi, 0, 0)),
                  pl.BlockSpec((d, n), lambda bi, i: (0, 0), pipeline_mode=pl.Buffered(1))],
        out_specs=pl.BlockSpec((1, tm, n), lambda bi, i: (bi, i, 0)),
        out_shape=jax.ShapeDtypeStruct((b, s, n), BF16),
        compiler_params=_cparams(("arbitrary", "arbitrary")),
        name="in_proj",
    )(x, mod, w)


def _store_values_transposed(v_ref, vt_ref):
    t = ATT_BLOCK
    for n in range(vt_ref.shape[0]):
        vt_ref[n] = v_ref[0, n * t:(n + 1) * t, :].astype(F32).T.astype(BF16)


def _head_rms(o0_t, o1_t, gain):
    def norm(o_t):
        ms = jnp.mean(o_t * o_t, axis=0, keepdims=True)
        return o_t * lax.rsqrt(ms + RMS_EPS)
    both = jnp.concatenate([norm(o0_t), norm(o1_t)], axis=0)
    return both.T * gain


def _masked_heads(q2):
    q2 = q2.astype(F32)
    lo = lax.broadcasted_iota(jnp.int32, q2.shape, 1) < HEAD_DIM
    return [jnp.where(lo, q2, 0.0).astype(BF16), jnp.where(lo, 0.0, q2).astype(BF16)]


def _sb_kernel(q_ref, k_ref, v_ref, g_ref, o_ref, vt_ref, acc_ref, z_ref, w_ref):
    t = ATT_BLOCK
    gi = pl.program_id(2)

    @pl.when(gi == 0)
    def _():
        _store_values_transposed(v_ref, vt_ref)

    qh = []
    for j in range(2):
        qh += _masked_heads(q_ref[0, j * t:(j + 1) * t, :])
    row = lax.broadcasted_iota(jnp.int32, (t, t), 0)
    col = lax.broadcasted_iota(jnp.int32, (t, t), 1)
    past = row < col
    tri = jnp.where(col >= row, 1.0, 0.0).astype(BF16)
    last_row = lax.broadcasted_iota(jnp.int32, (8, t), 0) == 7

    def key_block(r, j):
        return jnp.maximum(2 * gi + j - r, 0)

    def logits(r):
        for j in range(2):
            k2 = k_ref[0, pl.ds(pl.multiple_of(key_block(r, j) * t, t), t), :]
            for h in range(2):
                z_ref[r % 2, 2 * j + h] = _dot_nt(k2, qh[2 * j + h])

    def weights(r, carries, diag):
        out = []
        for s in range(4):
            carry = carries[s] if diag else jnp.where(2 * gi + s // 2 - r >= 0, carries[s], SB_PARKED)
            z = z_ref[r % 2, s]
            a = jnp.maximum(z, 0.0) + jnp.log(1.0 + jnp.exp2(jnp.abs(z) * -LOG2_E))
            if diag:
                a = jnp.where(past, a, 0.0)
            a = jnp.concatenate([a[:t - 8], a[t - 8:] + jnp.where(last_row, carry, 0.0)], axis=0)
            ainc = _dot(tri, a.astype(BF16))
            w = jnp.exp(z - ainc)
            if diag:
                w = jnp.where(past, w, 0.0)
            w_ref[r % 2, s] = w.astype(BF16)
            out.append(ainc[0:1, :])
        return out

    def values(r):
        for s in range(4):
            h = s % 2
            vt = vt_ref[key_block(r, s // 2), h * HEAD_DIM:(h + 1) * HEAD_DIM, :]
            acc_ref[s] += _dot(vt, w_ref[r % 2, s])

    def any_alive(carries, r_next):
        least = jnp.min(jnp.minimum(jnp.minimum(carries[0], carries[1]), jnp.minimum(carries[2], carries[3])))
        return jnp.logical_and(r_next <= 2 * gi + 1, least < SB_DEAD_LOG).astype(jnp.int32)

    acc_ref[...] = jnp.zeros(acc_ref.shape, F32)
    zero = jnp.zeros((1, t), F32)
    logits(0)
    logits(1)
    carries = weights(0, [zero] * 4, True)

    def body(st):
        r = st[1]
        values(r - 1)
        out = weights(r, st[2:], False)
        logits(r + 1)
        return (any_alive(out, r + 1), r + 1, *out)

    st = lax.while_loop(lambda st: st[0] > 0, body, (any_alive(carries, 1), jnp.int32(1), *carries))
    values(st[1] - 1)
    for j in range(2):
        o_ref[0, j * t:(j + 1) * t, :] = _head_rms(acc_ref[2 * j], acc_ref[2 * j + 1], g_ref[...]).astype(o_ref.dtype)


def _sb_attention(qkv, gain, n_pairs, q_blk0, k_blk0, v_blk0):
    b, s, _ = qkv.shape
    t = ATT_BLOCK
    return pl.pallas_call(
        _sb_kernel,
        grid=(b, n_pairs, s // (2 * t)),
        in_specs=[pl.BlockSpec((1, 2 * t, LANES), lambda bi, p, i: (bi, i, q_blk0 + p)),
                  pl.BlockSpec((1, s, LANES), lambda bi, p, i: (bi, 0, k_blk0 + p)),
                  pl.BlockSpec((1, s, LANES), lambda bi, p, i: (bi, 0, v_blk0 + p)),
                  pl.BlockSpec((1, LANES), lambda bi, p, i: (0, p))],
        out_specs=pl.BlockSpec((1, 2 * t, LANES), lambda bi, p, i: (bi, i, p)),
        out_shape=jax.ShapeDtypeStruct((b, s, n_pairs * LANES), BF16),
        scratch_shapes=[pltpu.VMEM((s // t, LANES, t), BF16),
                        pltpu.VMEM((4, HEAD_DIM, t), F32),
                        pltpu.VMEM((2, 4, t, t), F32),
                        pltpu.VMEM((2, 4, t, t), BF16)],
        compiler_params=_cparams(("arbitrary", "arbitrary", "arbitrary")),
        name="sb_attention",
    )(qkv, qkv, qkv, gain)


def _moba_gate_kernel(q_ref, k_ref, sel_ref):
    t = ATT_BLOCK
    s_len = k_ref.shape[1]
    nb = s_len // t
    q2 = q_ref[0]
    km = jnp.mean(k_ref[0].astype(F32).reshape(nb, t, LANES), axis=1)
    lane = lax.broadcasted_iota(jnp.int32, (nb, LANES), 1)
    blk = lax.broadcasted_iota(jnp.int32, (nb, s_len), 0)
    blk_f = blk.astype(F32)
    q_blk = lax.broadcasted_iota(jnp.int32, (nb, s_len), 1) // t
    fully_past = blk < q_blk
    for h in range(2):
        in_head = (lane < HEAD_DIM) if h == 0 else (lane >= HEAD_DIM)
        km_hi, km_lo = _split(jnp.where(in_head, km, 0.0))
        hi_lo = _dot_nt(jnp.concatenate([km_hi, km_lo], axis=0), q2)
        gate = hi_lo[:nb] + hi_lo[nb:]
        g = jnp.where(fully_past, gate, -jnp.inf)
        mask = jnp.full((nb, s_len), NEG_BIG, F32)
        for _ in range(MOBA_TOPK):
            mx = jnp.max(g, axis=0, keepdims=True)
            first = jnp.min(jnp.where(g == mx, blk_f, float(nb)), axis=0, keepdims=True)
            pick = blk_f == first
            mask = jnp.where(pick, 0.0, mask)
            g = jnp.where(pick, -jnp.inf, g)
        mask = jnp.where(fully_past, mask, NEG_BIG)
        sel_ref[0, 0, h] = jnp.where(blk == q_blk, 0.0, mask)


def _moba_gate(qkv, n_pairs, q_blk0, k_blk0):
    b, s, _ = qkv.shape
    nb = s // ATT_BLOCK
    return pl.pallas_call(
        _moba_gate_kernel,
        grid=(b, n_pairs),
        in_specs=[pl.BlockSpec((1, s, LANES), lambda bi, p: (bi, 0, q_blk0 + p)),
                  pl.BlockSpec((1, s, LANES), lambda bi, p: (bi, 0, k_blk0 + p))],
        out_specs=pl.BlockSpec((1, 1, 2, nb, s), lambda bi, p: (bi, p, 0, 0, 0)),
        out_shape=jax.ShapeDtypeStruct((b, n_pairs, 2, nb, s), F32),
        compiler_params=_cparams(("arbitrary", "arbitrary")),
        name="moba_gate",
    )(qkv, qkv)


def _moba_kernel(q_ref, k_ref, v_ref, sel_ref, bias_ref, g_ref, o_ref, vt_ref, s_ref, acc_ref):
    t = ATT_BLOCK
    gi = pl.program_id(2)

    @pl.when(gi == 0)
    def _():
        ones_row = jnp.where(lax.broadcasted_iota(jnp.int32, (MOBA_PAD_ROWS, t), 0) == 0, 1.0, 0.0).astype(BF16)
        for n in range(vt_ref.shape[0]):
            vt = v_ref[0, n * t:(n + 1) * t, :].astype(F32).T.astype(BF16)
            for h in range(2):
                vt_ref[n, h, :HEAD_DIM] = vt[h * HEAD_DIM:(h + 1) * HEAD_DIM]
                vt_ref[n, h, HEAD_DIM:] = ones_row

    qs = []
    for j in range(2):
        qs += _masked_heads(q_ref[0, j * t:(j + 1) * t, :])

    def tile_kind(n, j):
        return jnp.clip(n - (2 * gi + j) + 2, 0, 3)

    def score_blocks(first, count, mx):
        mx = list(mx)
        start = first * t if isinstance(first, int) else pl.multiple_of(first * t, t)
        keys = k_ref[0, pl.ds(start, count * t), :]
        for slot in range(4):
            j, h = slot // 2, slot % 2
            qk = _dot_nt(keys, qs[slot])
            for c in range(count):
                n = first + c
                s = qk[c * t:(c + 1) * t] + bias_ref[h, tile_kind(n, j)]
                s = s + sel_ref[0, 0, h, pl.ds(n, 1), j * t:(j + 1) * t]
                s_ref[slot, n] = s
                mx[slot] = jnp.maximum(mx[slot], jnp.max(s, axis=0, keepdims=True))
        return tuple(mx)

    def weigh_blocks(first, count, mx):
        blocks = [first + c for c in range(count)]
        for slot in range(4):
            p = [jnp.exp2((s_ref[slot, n] - mx[slot]).astype(BF16)) for n in blocks]
            vt = [vt_ref[n, slot % 2] for n in blocks]
            acc_ref[slot] += _dot(jnp.concatenate(vt, axis=1), jnp.concatenate(p, axis=0))

    n_blocks = 2 * gi + 2
    lead = n_blocks % 4
    mx = (jnp.full((1, t), -jnp.inf, F32),) * 4
    mx = lax.cond(lead > 0, lambda m: score_blocks(0, 2, m), lambda m: m, mx)
    mx = lax.fori_loop(0, n_blocks // 4, lambda i, m: score_blocks(lead + 4 * i, 4, m), mx)

    acc_ref[...] = jnp.zeros(acc_ref.shape, F32)

    @pl.when(lead > 0)
    def _():
        weigh_blocks(0, 2, mx)

    def weigh_trip(i, c):
        weigh_blocks(lead + 4 * i, 4, mx)
        return c

    lax.fori_loop(0, n_blocks // 4, weigh_trip, 0)
    for j in range(2):
        o = [acc_ref[2 * j + h, :HEAD_DIM] / acc_ref[2 * j + h, HEAD_DIM:HEAD_DIM + 1] for h in range(2)]
        o_ref[0, j * t:(j + 1) * t, :] = _head_rms(o[0], o[1], g_ref[...]).astype(o_ref.dtype)


def _moba_attention(qkv, sel, bias, gain, n_pairs, q_blk0, k_blk0, v_blk0):
    b, s, _ = qkv.shape
    t = ATT_BLOCK
    nb = s // t
    return pl.pallas_call(
        _moba_kernel,
        grid=(b, n_pairs, nb // 2),
        in_specs=[pl.BlockSpec((1, 2 * t, LANES), lambda bi, p, i: (bi, i, q_blk0 + p)),
                  pl.BlockSpec((1, s, LANES), lambda bi, p, i: (bi, 0, k_blk0 + p)),
                  pl.BlockSpec((1, s, LANES), lambda bi, p, i: (bi, 0, v_blk0 + p)),
                  pl.BlockSpec((1, 1, 2, nb, 2 * t), lambda bi, p, i: (bi, p, 0, 0, i)),
                  pl.BlockSpec((2, 4, t, t), lambda bi, p, i: (p, 0, 0, 0)),
                  pl.BlockSpec((1, LANES), lambda bi, p, i: (0, p))],
        out_specs=pl.BlockSpec((1, 2 * t, LANES), lambda bi, p, i: (bi, i, p)),
        out_shape=jax.ShapeDtypeStruct((b, s, n_pairs * LANES), BF16),
        scratch_shapes=[pltpu.VMEM((nb, 2, HEAD_DIM + MOBA_PAD_ROWS, t), BF16),
                        pltpu.VMEM((4, nb, t, t), F32),
                        pltpu.VMEM((4, HEAD_DIM + MOBA_PAD_ROWS, t), F32)],
        compiler_params=_cparams(("arbitrary", "arbitrary", "arbitrary")),
        name="moba_attention",
    )(qkv, qkv, qkv, sel, bias, gain)


def _mlp_kernel(osb_ref, omb_ref, x_ref, mod_ref, w1_ref, w2_ref, g1_ref, b1_ref,
                wg_ref, wu_ref, wd_ref, g2_ref, b2_ref, o_ref, *, alpha, chunks):
    y = _dot(osb_ref[0], w1_ref[...]) + _dot(omb_ref[0], w2_ref[...])
    x1 = _layer_norm(alpha * x_ref[0] + mod_ref[0, 2:3, :] * y, g1_ref[...], b1_ref[...])
    u = (x1 * (1.0 + mod_ref[0, 4:5, :]) + mod_ref[0, 3:4, :]).astype(BF16)
    f = None
    for c0, c1 in chunks:
        hg = _dot(u, wg_ref[:, c0:c1])
        hu = _dot(u, wu_ref[:, c0:c1])
        hidden = (hg * jax.nn.sigmoid(hg) * hu).astype(BF16)
        part = _dot(hidden, wd_ref[c0:c1, :])
        f = part if f is None else f + part
    o_ref[0] = _layer_norm(alpha * x1 + mod_ref[0, 5:6, :] * f, g2_ref[...], b2_ref[...])


def _hidden_chunks(d_ff):
    edges = list(range(0, d_ff, FF_CHUNK)) + [d_ff]
    return tuple(zip(edges[:-1], edges[1:]))


def _mlp(o_sb, o_mb, x, mod, w1, w2, ln1_g, ln1_b, wg, wu, wd, ln2_g, ln2_b, alpha, tm):
    b, s, d = x.shape
    h1, h2 = w1.shape[0], w2.shape[0]
    d_ff = wg.shape[1]
    const = lambda bi, i: (0, 0)
    resident = lambda shape: pl.BlockSpec(shape, const, pipeline_mode=pl.Buffered(1))
    return pl.pallas_call(
        functools.partial(_mlp_kernel, alpha=alpha, chunks=_hidden_chunks(d_ff)),
        grid=(b, s // tm),
        in_specs=[pl.BlockSpec((1, tm, h1), lambda bi, i: (bi, i, 0)),
                  pl.BlockSpec((1, tm, h2), lambda bi, i: (bi, i, 0)),
                  pl.BlockSpec((1, tm, d), lambda bi, i: (bi, i, 0)),
                  pl.BlockSpec((1, 6, d), lambda bi, i: (bi, 0, 0)),
                  resident((h1, d)), resident((h2, d)), resident((1, d)), resident((1, d)),
                  resident((d, d_ff)), resident((d, d_ff)), resident((d_ff, d)),
                  resident((1, d)), resident((1, d))],
        out_specs=pl.BlockSpec((1, tm, d), lambda bi, i: (bi, i, 0)),
        out_shape=jax.ShapeDtypeStruct((b, s, d), F32),
        compiler_params=_cparams(("arbitrary", "arbitrary")),
        name="out_proj_ffn",
    )(o_sb, o_mb, x, mod, w1, w2, ln1_g, ln1_b, wg, wu, wd, ln2_g, ln2_b)


def _t5_bucket(dist):
    n = np.maximum(dist, 0)
    max_exact = N_BUCKETS // 2
    nf = np.maximum(n, 1).astype(np.float32)
    large = max_exact + (np.log(nf / np.float32(max_exact)) / np.float32(math.log(MAX_DISTANCE / max_exact))
                         * np.float32(N_BUCKETS - max_exact)).astype(np.int32)
    large = np.minimum(large, N_BUCKETS - 1)
    return np.where(n < max_exact, n, large)


def _bucket_lookup(rel_h, bucket):
    out = jnp.zeros((rel_h.shape[0],) + bucket.shape, F32)
    for b in range(N_BUCKETS):
        out = jnp.where((bucket == b)[None], rel_h[:, b].reshape((-1,) + (1,) * bucket.ndim), out)
    return out


def _moba_bias_tiles(rel_bias):
    t = ATT_BLOCK
    j = np.arange(t)[:, None]
    i = np.arange(t)[None, :]
    rel_h = rel_bias.T.astype(F32) * LOG2_E
    far = jnp.broadcast_to(rel_h[:, int(_t5_bucket(np.asarray(2 * t)))][:, None, None], (rel_h.shape[0], t, t))
    prev = _bucket_lookup(rel_h, _t5_bucket(t + i - j))
    own = jnp.where((j <= i)[None], _bucket_lookup(rel_h, _t5_bucket(i - j)), NEG_BIG)
    return jnp.stack([far, prev, own, jnp.full_like(far, NEG_BIG)], axis=1)


def kernel(x, c, w_ada, b_ada, w_in, sb_gain, mb_gain, w_out, ln1_g, ln1_b,
           w_gate, w_up, w_down, ln2_g, ln2_b, rel_bias):
    depth = w_ada.shape[0]
    b, s, d = x.shape
    d_sb = sb_gain.shape[1]
    d_mb = mb_gain.shape[1]
    assert ATT_BLOCK == MOBA_BLOCK and MOBA_BLOCK >= 2 * MAX_DISTANCE
    assert s % ATT_BLOCK == 0 and (s // ATT_BLOCK) % 8 == 0
    assert d_sb % LANES == 0 and d_mb % LANES == 0 and d % LANES == 0
    alpha = float((2 * depth) ** 0.25)
    tm = min(512, s)
    sb_pairs, mb_pairs = d_sb // LANES, d_mb // LANES

    c_pad = jnp.pad(c, ((0, (-c.shape[0]) % 8), (0, 0)))
    bias_tiles = _moba_bias_tiles(rel_bias)
    col_scale = jnp.concatenate([jnp.full((d_sb,), HEAD_DIM ** -0.5, F32), jnp.ones((2 * d_sb,), F32),
                                 jnp.full((d_mb,), HEAD_DIM ** -0.5 * LOG2_E, F32), jnp.ones((2 * d_mb,), F32)])
    mb0 = 3 * sb_pairs

    for l in range(depth):
        mod = _modulation(c_pad, w_ada[l], b_ada[l][None, :])[:b].reshape(b, 6, d)
        qkv = _in_projection(x, mod, (w_in[l] * col_scale[None, :]).astype(BF16), tm)

        o_sb = _sb_attention(qkv, sb_gain[l][None, :], sb_pairs, 0, sb_pairs, 2 * sb_pairs)
        sel = _moba_gate(qkv, mb_pairs, mb0, mb0 + mb_pairs)
        o_mb = _moba_attention(qkv, sel, bias_tiles, mb_gain[l][None, :], mb_pairs,
                               mb0, mb0 + mb_pairs, mb0 + 2 * mb_pairs)

        wo = w_out[l].astype(BF16)
        x = _mlp(o_sb, o_mb, x, mod, wo[:d_sb], wo[d_sb:], ln1_g[l][None, :], ln1_b[l][None, :],
                 w_gate[l].astype(BF16), w_up[l].astype(BF16), w_down[l].astype(BF16),
                 ln2_g[l][None, :], ln2_b[l][None, :], alpha, tm)
    return x
```

```python
import functools
import math

import jax
import jax.numpy as jnp
import numpy as np
from jax import lax
from jax.experimental import pallas as pl
from jax.experimental.pallas import tpu as pltpu

HEAD_DIM = 64
MOBA_BLOCK = 256
MOBA_TOPK = 3
N_BUCKETS = 32
MAX_DISTANCE = 128
LN_EPS = 1e-5
RMS_EPS = 1e-6
NEG_BIG = -1e30
SB_DEAD_LOG = 120.0
LOG2_E = 1.4426950408889634
FF_CHUNK = 1024
MOBA_PAD_ROWS = 16
SB_PARKED = 1e30
LANES = 128
ATT_BLOCK = 256
VMEM_LIMIT = 56 * 1024 * 1024

F32 = jnp.float32
BF16 = jnp.bfloat16


def _cparams(sem):
    return pltpu.CompilerParams(dimension_semantics=sem, vmem_limit_bytes=VMEM_LIMIT)


def _split(a):
    hi = a.astype(BF16)
    lo = (a - hi.astype(F32)).astype(BF16)
    return hi, lo


def _dot(a, b):
    return jnp.dot(a, b, preferred_element_type=F32)


def _dot_nt(a, b):
    return lax.dot_general(a, b, (((1,), (1,)), ((), ())), preferred_element_type=F32)


def _layer_norm(r, g, b):
    mu = jnp.mean(r, axis=-1, keepdims=True)
    d = r - mu
    var = jnp.mean(d * d, axis=-1, keepdims=True)
    return d * lax.rsqrt(var + LN_EPS) * g + b


def _mod_kernel(c_ref, w_ref, b_ref, o_ref):
    c = c_ref[...]
    cond = c * jax.nn.sigmoid(c)
    c_hi, c_lo = _split(cond)
    w_hi, w_lo = _split(w_ref[...])
    o_ref[...] = _dot(c_hi, w_hi) + _dot(c_hi, w_lo) + _dot(c_lo, w_hi) + b_ref[...]


def _modulation(c_pad, w, b):
    rows, d = c_pad.shape
    n = w.shape[1]
    tn = n // 4 if n % (4 * LANES) == 0 else n
    return pl.pallas_call(
        _mod_kernel,
        grid=(n // tn,),
        in_specs=[pl.BlockSpec((rows, d), lambda j: (0, 0)),
                  pl.BlockSpec((d, tn), lambda j: (0, j)),
                  pl.BlockSpec((1, tn), lambda j: (0, j))],
        out_specs=pl.BlockSpec((rows, tn), lambda j: (0, j)),
        out_shape=jax.ShapeDtypeStruct((rows, n), F32),
        compiler_params=_cparams(("arbitrary",)),
        name="adaln_mod",
    )(c_pad, w, b)


def _inproj_kernel(x_ref, mod_ref, w_ref, o_ref):
    u = x_ref[0] * (1.0 + mod_ref[0, 1:2, :]) + mod_ref[0, 0:1, :]
    o_ref[0] = _dot(u.astype(BF16), w_ref[...]).astype(BF16)


def _in_projection(x, mod, w, tm):
    b, s, d = x.shape
    n = w.shape[1]
    return pl.pallas_call(
        _inproj_kernel,
        grid=(b, s // tm),
        in_specs=[pl.BlockSpec((1, tm, d), lambda bi, i: (bi, i, 0)),
                  pl.BlockSpec((1, 6, d), lambda bi, i: (bi, 0, 0)),
                  pl.BlockSpec((d, n), lambda bi, i: (0, 0), pipeline_mode=pl.Buffered(1))],
        out_specs=pl.BlockSpec((1, tm, n), lambda bi, i: (bi, i, 0)),
        out_shape=jax.ShapeDtypeStruct((b, s, n), BF16),
        compiler_params=_cparams(("arbitrary", "arbitrary")),
        name="in_proj",
    )(x, mod, w)


def _store_values_transposed(v_ref, vt_ref):
    t = ATT_BLOCK
    for n in range(vt_ref.shape[0]):
        vt_ref[n] = v_ref[0, n * t:(n + 1) * t, :].astype(F32).T.astype(BF16)


def _head_rms(o0_t, o1_t, gain):
    def norm(o_t):
        ms = jnp.mean(o_t * o_t, axis=0, keepdims=True)
        return o_t * lax.rsqrt(ms + RMS_EPS)
    both = jnp.concatenate([norm(o0_t), norm(o1_t)], axis=0)
    return both.T * gain


def _masked_heads(q2):
    q2 = q2.astype(F32)
    lo = lax.broadcasted_iota(jnp.int32, q2.shape, 1) < HEAD_DIM
    return [jnp.where(lo, q2, 0.0).astype(BF16), jnp.where(lo, 0.0, q2).astype(BF16)]


def _sb_kernel(q_ref, k_ref, v_ref, g_ref, o_ref, vt_ref, acc_ref, z_ref, w_ref):
    t = ATT_BLOCK
    gi = pl.program_id(2)

    @pl.when(gi == 0)
    def _():
        _store_values_transposed(v_ref, vt_ref)

    qh = []
    for j in range(2):
        qh += _masked_heads(q_ref[0, j * t:(j + 1) * t, :])
    row = lax.broadcasted_iota(jnp.int32, (t, t), 0)
    col = lax.broadcasted_iota(jnp.int32, (t, t), 1)
    past = row < col
    tri = jnp.where(col >= row, 1.0, 0.0).astype(BF16)
    last_row = lax.broadcasted_iota(jnp.int32, (8, t), 0) == 7

    def key_block(r, j):
        return jnp.maximum(2 * gi + j - r, 0)

    def logits(r):
        for j in range(2):
            k2 = k_ref[0, pl.ds(pl.multiple_of(key_block(r, j) * t, t), t), :]
            for h in range(2):
                z_ref[r % 2, 2 * j + h] = _dot_nt(k2, qh[2 * j + h])

    def weights(r, carries, diag):
        out = []
        for s in range(4):
            carry = carries[s] if diag else jnp.where(2 * gi + s // 2 - r >= 0, carries[s], SB_PARKED)
            z = z_ref[r % 2, s]
            a = jnp.maximum(z, 0.0) + jnp.log(1.0 + jnp.exp2(jnp.abs(z) * -LOG2_E))
            if diag:
                a = jnp.where(past, a, 0.0)
            a = jnp.concatenate([a[:t - 8], a[t - 8:] + jnp.where(last_row, carry, 0.0)], axis=0)
            ainc = _dot(tri, a.astype(BF16))
            w = jnp.exp(z - ainc)
            if diag:
                w = jnp.where(past, w, 0.0)
            w_ref[r % 2, s] = w.astype(BF16)
            out.append(ainc[0:1, :])
        return out

    def values(r):
        for s in range(4):
            h = s % 2
            vt = vt_ref[key_block(r, s // 2), h * HEAD_DIM:(h + 1) * HEAD_DIM, :]
            acc_ref[s] += _dot(vt, w_ref[r % 2, s])

    def any_alive(carries, r_next):
        least = jnp.min(jnp.minimum(jnp.minimum(carries[0], carries[1]), jnp.minimum(carries[2], carries[3])))
        return jnp.logical_and(r_next <= 2 * gi + 1, least < SB_DEAD_LOG).astype(jnp.int32)

    acc_ref[...] = jnp.zeros(acc_ref.shape, F32)
    zero = jnp.zeros((1, t), F32)
    logits(0)
    logits(1)
    carries = weights(0, [zero] * 4, True)

    def body(st):
        r = st[1]
        values(r - 1)
        out = weights(r, st[2:], False)
        logits(r + 1)
        return (any_alive(out, r + 1), r + 1, *out)

    st = lax.while_loop(lambda st: st[0] > 0, body, (any_alive(carries, 1), jnp.int32(1), *carries))
    values(st[1] - 1)
    for j in range(2):
        o_ref[0, j * t:(j + 1) * t, :] = _head_rms(acc_ref[2 * j], acc_ref[2 * j + 1], g_ref[...]).astype(o_ref.dtype)


def _sb_attention(qkv, gain, n_pairs, q_blk0, k_blk0, v_blk0):
    b, s, _ = qkv.shape
    t = ATT_BLOCK
    return pl.pallas_call(
        _sb_kernel,
        grid=(b, n_pairs, s // (2 * t)),
        in_specs=[pl.BlockSpec((1, 2 * t, LANES), lambda bi, p, i: (bi, i, q_blk0 + p)),
                  pl.BlockSpec((1, s, LANES), lambda bi, p, i: (bi, 0, k_blk0 + p)),
                  pl.BlockSpec((1, s, LANES), lambda bi, p, i: (bi, 0, v_blk0 + p)),
                  pl.BlockSpec((1, LANES), lambda bi, p, i: (0, p))],
        out_specs=pl.BlockSpec((1, 2 * t, LANES), lambda bi, p, i: (bi, i, p)),
        out_shape=jax.ShapeDtypeStruct((b, s, n_pairs * LANES), BF16),
        scratch_shapes=[pltpu.VMEM((s // t, LANES, t), BF16),
                        pltpu.VMEM((4, HEAD_DIM, t), F32),
                        pltpu.VMEM((2, 4, t, t), F32),
                        pltpu.VMEM((2, 4, t, t), BF16)],
        compiler_params=_cparams(("arbitrary", "arbitrary", "arbitrary")),
        name="sb_attention",
    )(qkv, qkv, qkv, gain)


def _moba_gate_kernel(q_ref, k_ref, sel_ref):
    t = ATT_BLOCK
    s_len = k_ref.shape[1]
    nb = s_len // t
    q2 = q_ref[0]
    km = jnp.mean(k_ref[0].astype(F32).reshape(nb, t, LANES), axis=1)
    lane = lax.broadcasted_iota(jnp.int32, (nb, LANES), 1)
    blk = lax.broadcasted_iota(jnp.int32, (nb, s_len), 0)
    blk_f = blk.astype(F32)
    q_blk = lax.broadcasted_iota(jnp.int32, (nb, s_len), 1) // t
    fully_past = blk < q_blk
    for h in range(2):
        in_head = (lane < HEAD_DIM) if h == 0 else (lane >= HEAD_DIM)
        km_hi, km_lo = _split(jnp.where(in_head, km, 0.0))
        hi_lo = _dot_nt(jnp.concatenate([km_hi, km_lo], axis=0), q2)
        gate = hi_lo[:nb] + hi_lo[nb:]
        g = jnp.where(fully_past, gate, -jnp.inf)
        mask = jnp.full((nb, s_len), NEG_BIG, F32)
        for _ in range(MOBA_TOPK):
            mx = jnp.max(g, axis=0, keepdims=True)
            first = jnp.min(jnp.where(g == mx, blk_f, float(nb)), axis=0, keepdims=True)
            pick = blk_f == first
            mask = jnp.where(pick, 0.0, mask)
            g = jnp.where(pick, -jnp.inf, g)
        mask = jnp.where(fully_past, mask, NEG_BIG)
        sel_ref[0, 0, h] = jnp.where(blk == q_blk, 0.0, mask)


def _moba_gate(qkv, n_pairs, q_blk0, k_blk0):
    b, s, _ = qkv.shape
    nb = s // ATT_BLOCK
    return pl.pallas_call(
        _moba_gate_kernel,
        grid=(b, n_pairs),
        in_specs=[pl.BlockSpec((1, s, LANES), lambda bi, p: (bi, 0, q_blk0 + p)),
                  pl.BlockSpec((1, s, LANES), lambda bi, p: (bi, 0, k_blk0 + p))],
        out_specs=pl.BlockSpec((1, 1, 2, nb, s), lambda bi, p: (bi, p, 0, 0, 0)),
        out_shape=jax.ShapeDtypeStruct((b, n_pairs, 2, nb, s), F32),
        compiler_params=_cparams(("arbitrary", "arbitrary")),
        name="moba_gate",
    )(qkv, qkv)


def _moba_kernel(q_ref, k_ref, v_ref, sel_ref, bias_ref, g_ref, o_ref, vt_ref, s_ref, acc_ref):
    t = ATT_BLOCK
    gi = pl.program_id(2)

    @pl.when(gi == 0)
    def _():
        ones_row = jnp.where(lax.broadcasted_iota(jnp.int32, (MOBA_PAD_ROWS, t), 0) == 0, 1.0, 0.0).astype(BF16)
        for n in range(vt_ref.shape[0]):
            vt = v_ref[0, n * t:(n + 1) * t, :].astype(F32).T.astype(BF16)
            for h in range(2):
                vt_ref[n, h, :HEAD_DIM] = vt[h * HEAD_DIM:(h + 1) * HEAD_DIM]
                vt_ref[n, h, HEAD_DIM:] = ones_row

    qs = []
    for j in range(2):
        qs += _masked_heads(q_ref[0, j * t:(j + 1) * t, :])

    def tile_kind(n, j):
        return jnp.clip(n - (2 * gi + j) + 2, 0, 3)

    def score_blocks(first, count, mx):
        mx = list(mx)
        start = first * t if isinstance(first, int) else pl.multiple_of(first * t, t)
        keys = k_ref[0, pl.ds(start, count * t), :]
        for slot in range(4):
            j, h = slot // 2, slot % 2
            qk = _dot_nt(keys, qs[slot])
            for c in range(count):
                n = first + c
                s = qk[c * t:(c + 1) * t] + bias_ref[h, tile_kind(n, j)]
                s = s + sel_ref[0, 0, h, pl.ds(n, 1), j * t:(j + 1) * t]
                s_ref[slot, n] = s
                mx[slot] = jnp.maximum(mx[slot], jnp.max(s, axis=0, keepdims=True))
        return tuple(mx)

    def weigh_blocks(first, count, mx):
        blocks = [first + c for c in range(count)]
        for slot in range(4):
            p = [jnp.exp2((s_ref[slot, n] - mx[slot]).astype(BF16)) for n in blocks]
            vt = [vt_ref[n, slot % 2] for n in blocks]
            acc_ref[slot] += _dot(jnp.concatenate(vt, axis=1), jnp.concatenate(p, axis=0))

    n_blocks = 2 * gi + 2
    lead = n_blocks % 4
    mx = (jnp.full((1, t), -jnp.inf, F32),) * 4
    mx = lax.cond(lead > 0, lambda m: score_blocks(0, 2, m), lambda m: m, mx)
    mx = lax.fori_loop(0, n_blocks // 4, lambda i, m: score_blocks(lead + 4 * i, 4, m), mx)

    acc_ref[...] = jnp.zeros(acc_ref.shape, F32)

    @pl.when(lead > 0)
    def _():
        weigh_blocks(0, 2, mx)

    def weigh_trip(i, c):
        weigh_blocks(lead + 4 * i, 4, mx)
        return c

    lax.fori_loop(0, n_blocks // 4, weigh_trip, 0)
    for j in range(2):
        o = [acc_ref[2 * j + h, :HEAD_DIM] / acc_ref[2 * j + h, HEAD_DIM:HEAD_DIM + 1] for h in range(2)]
        o_ref[0, j * t:(j + 1) * t, :] = _head_rms(o[0], o[1], g_ref[...]).astype(o_ref.dtype)


def _moba_attention(qkv, sel, bias, gain, n_pairs, q_blk0, k_blk0, v_blk0):
    b, s, _ = qkv.shape
    t = ATT_BLOCK
    nb = s // t
    return pl.pallas_call(
        _moba_kernel,
        grid=(b, n_pairs, nb // 2),
        in_specs=[pl.BlockSpec((1, 2 * t, LANES), lambda bi, p, i: (bi, i, q_blk0 + p)),
                  pl.BlockSpec((1, s, LANES), lambda bi, p, i: (bi, 0, k_blk0 + p)),
                  pl.BlockSpec((1, s, LANES), lambda bi, p, i: (bi, 0, v_blk0 + p)),
                  pl.BlockSpec((1, 1, 2, nb, 2 * t), lambda bi, p, i: (bi, p, 0, 0, i)),
                  pl.BlockSpec((2, 4, t, t), lambda bi, p, i: (p, 0, 0, 0)),
                  pl.BlockSpec((1, LANES), lambda bi, p, i: (0, p))],
        out_specs=pl.BlockSpec((1, 2 * t, LANES), lambda bi, p, i: (bi, i, p)),
        out_shape=jax.ShapeDtypeStruct((b, s, n_pairs * LANES), BF16),
        scratch_shapes=[pltpu.VMEM((nb, 2, HEAD_DIM + MOBA_PAD_ROWS, t), BF16),
                        pltpu.VMEM((4, nb, t, t), F32),
                        pltpu.VMEM((4, HEAD_DIM + MOBA_PAD_ROWS, t), F32)],
        compiler_params=_cparams(("arbitrary", "arbitrary", "arbitrary")),
        name="moba_attention",
    )(qkv, qkv, qkv, sel, bias, gain)


def _mlp_kernel(osb_ref, omb_ref, x_ref, mod_ref, w1_ref, w2_ref, g1_ref, b1_ref,
                wg_ref, wu_ref, wd_ref, g2_ref, b2_ref, o_ref, *, alpha, chunks):
    y = _dot(osb_ref[0], w1_ref[...]) + _dot(omb_ref[0], w2_ref[...])
    x1 = _layer_norm(alpha * x_ref[0] + mod_ref[0, 2:3, :] * y, g1_ref[...], b1_ref[...])
    u = (x1 * (1.0 + mod_ref[0, 4:5, :]) + mod_ref[0, 3:4, :]).astype(BF16)
    f = None
    for c0, c1 in chunks:
        hg = _dot(u, wg_ref[:, c0:c1])
        hu = _dot(u, wu_ref[:, c0:c1])
        hidden = (hg * jax.nn.sigmoid(hg) * hu).astype(BF16)
        part = _dot(hidden, wd_ref[c0:c1, :])
        f = part if f is None else f + part
    o_ref[0] = _layer_norm(alpha * x1 + mod_ref[0, 5:6, :] * f, g2_ref[...], b2_ref[...])


def _hidden_chunks(d_ff):
    edges = list(range(0, d_ff, FF_CHUNK)) + [d_ff]
    return tuple(zip(edges[:-1], edges[1:]))


def _mlp(o_sb, o_mb, x, mod, w1, w2, ln1_g, ln1_b, wg, wu, wd, ln2_g, ln2_b, alpha, tm):
    b, s, d = x.shape
    h1, h2 = w1.shape[0], w2.shape[0]
    d_ff = wg.shape[1]
    const = lambda bi, i: (0, 0)
    resident = lambda shape: pl.BlockSpec(shape, const, pipeline_mode=pl.Buffered(1))
    return pl.pallas_call(
        functools.partial(_mlp_kernel, alpha=alpha, chunks=_hidden_chunks(d_ff)),
        grid=(b, s // tm),
        in_specs=[pl.BlockSpec((1, tm, h1), lambda bi, i: (bi, i, 0)),
                  pl.BlockSpec((1, tm, h2), lambda bi, i: (bi, i, 0)),
                  pl.BlockSpec((1, tm, d), lambda bi, i: (bi, i, 0)),
                  pl.BlockSpec((1, 6, d), lambda bi, i: (bi, 0, 0)),
                  resident((h1, d)), resident((h2, d)), resident((1, d)), resident((1, d)),
                  resident((d, d_ff)), resident((d, d_ff)), resident((d_ff, d)),
                  resident((1, d)), resident((1, d))],
        out_specs=pl.BlockSpec((1, tm, d), lambda bi, i: (bi, i, 0)),
        out_shape=jax.ShapeDtypeStruct((b, s, d), F32),
        compiler_params=_cparams(("arbitrary", "arbitrary")),
        name="out_proj_ffn",
    )(o_sb, o_mb, x, mod, w1, w2, ln1_g, ln1_b, wg, wu, wd, ln2_g, ln2_b)


def _t5_bucket(dist):
    n = np.maximum(dist, 0)
    max_exact = N_BUCKETS // 2
    nf = np.maximum(n, 1).astype(np.float32)
    large = max_exact + (np.log(nf / np.float32(max_exact)) / np.float32(math.log(MAX_DISTANCE / max_exact))
                         * np.float32(N_BUCKETS - max_exact)).astype(np.int32)
    large = np.minimum(large, N_BUCKETS - 1)
    return np.where(n < max_exact, n, large)


def _bucket_lookup(rel_h, bucket):
    out = jnp.zeros((rel_h.shape[0],) + bucket.shape, F32)
    for b in range(N_BUCKETS):
        out = jnp.where((bucket == b)[None], rel_h[:, b].reshape((-1,) + (1,) * bucket.ndim), out)
    return out


def _moba_bias_tiles(rel_bias):
    t = ATT_BLOCK
    j = np.arange(t)[:, None]
    i = np.arange(t)[None, :]
    rel_h = rel_bias.T.astype(F32) * LOG2_E
    n_heads = rel_h.shape[0]
    by_dist = _bucket_lookup(rel_h, _t5_bucket(np.arange(2 * t)))

    def toeplitz(v):
        return jnp.tile(v, (1, t))[:, :t * (2 * t - 1)].reshape(n_heads, t, 2 * t - 1)[:, :, :t]

    far = jnp.broadcast_to(by_dist[:, 2 * t - 1][:, None, None], (n_heads, t, t))
    prev = toeplitz(jnp.roll(by_dist, -t, axis=1))
    own = jnp.where((j <= i)[None], toeplitz(by_dist), NEG_BIG)
    return jnp.stack([far, prev, own, jnp.full_like(far, NEG_BIG)], axis=1)


def kernel(x, c, w_ada, b_ada, w_in, sb_gain, mb_gain, w_out, ln1_g, ln1_b,
           w_gate, w_up, w_down, ln2_g, ln2_b, rel_bias):
    depth = w_ada.shape[0]
    b, s, d = x.shape
    d_sb = sb_gain.shape[1]
    d_mb = mb_gain.shape[1]
    assert ATT_BLOCK == MOBA_BLOCK and MOBA_BLOCK >= 2 * MAX_DISTANCE
    assert s % ATT_BLOCK == 0 and (s // ATT_BLOCK) % 8 == 0
    assert d_sb % LANES == 0 and d_mb % LANES == 0 and d % LANES == 0
    alpha = float((2 * depth) ** 0.25)
    tm = min(512, s)
    sb_pairs, mb_pairs = d_sb // LANES, d_mb // LANES

    c_pad = jnp.pad(c, ((0, (-c.shape[0]) % 8), (0, 0)))
    bias_tiles = _moba_bias_tiles(rel_bias)
    col_scale = jnp.concatenate([jnp.full((d_sb,), HEAD_DIM ** -0.5, F32), jnp.ones((2 * d_sb,), F32),
                                 jnp.full((d_mb,), HEAD_DIM ** -0.5 * LOG2_E, F32), jnp.ones((2 * d_mb,), F32)])
    mb0 = 3 * sb_pairs

    for l in range(depth):
        mod = _modulation(c_pad, w_ada[l], b_ada[l][None, :])[:b].reshape(b, 6, d)
        qkv = _in_projection(x, mod, (w_in[l] * col_scale[None, :]).astype(BF16), min(2 * tm, s))

        o_sb = _sb_attention(qkv, sb_gain[l][None, :], sb_pairs, 0, sb_pairs, 2 * sb_pairs)
        sel = _moba_gate(qkv, mb_pairs, mb0, mb0 + mb_pairs)
        o_mb = _moba_attention(qkv, sel, bias_tiles, mb_gain[l][None, :], mb_pairs,
                               mb0, mb0 + mb_pairs, mb0 + 2 * mb_pairs)

        wo = w_out[l].astype(BF16)
        x = _mlp(o_sb, o_mb, x, mod, wo[:d_sb], wo[d_sb:], ln1_g[l][None, :], ln1_b[l][None, :],
                 w_gate[l].astype(BF16), w_up[l].astype(BF16), w_down[l].astype(BF16),
                 ln2_g[l][None, :], ln2_b[l][None, :], alpha, tm)
    return x
```

```python
import functools
import math

import jax
import jax.numpy as jnp
import numpy as np
from jax import lax
from jax.experimental import pallas as pl
from jax.experimental.pallas import tpu as pltpu

HEAD_DIM = 64
MOBA_BLOCK = 256
MOBA_TOPK = 3
N_BUCKETS = 32
MAX_DISTANCE = 128
LN_EPS = 1e-5
RMS_EPS = 1e-6
NEG_BIG = -1e30
SB_DEAD_LOG = 120.0
LOG2_E = 1.4426950408889634
FF_CHUNK = 1024
MOBA_PAD_ROWS = 16
SB_PARKED = 1e30
LANES = 128
ATT_BLOCK = 256
VMEM_LIMIT = 56 * 1024 * 1024

F32 = jnp.float32
BF16 = jnp.bfloat16


def _cparams(sem):
    return pltpu.CompilerParams(dimension_semantics=sem, vmem_limit_bytes=VMEM_LIMIT)


def _split(a):
    hi = a.astype(BF16)
    lo = (a - hi.astype(F32)).astype(BF16)
    return hi, lo


def _dot(a, b):
    return jnp.dot(a, b, preferred_element_type=F32)


def _dot_nt(a, b):
    return lax.dot_general(a, b, (((1,), (1,)), ((), ())), preferred_element_type=F32)


def _layer_norm(r, g, b):
    mu = jnp.mean(r, axis=-1, keepdims=True)
    d = r - mu
    var = jnp.mean(d * d, axis=-1, keepdims=True)
    return d * lax.rsqrt(var + LN_EPS) * g + b


def _mod_kernel(c_ref, w_ref, b_ref, o_ref):
    c = c_ref[...]
    cond = c * jax.nn.sigmoid(c)
    c_hi, c_lo = _split(cond)
    w_hi, w_lo = _split(w_ref[...])
    o_ref[...] = _dot(c_hi, w_hi) + _dot(c_hi, w_lo) + _dot(c_lo, w_hi) + b_ref[...]


def _modulation(c_pad, w, b):
    rows, d = c_pad.shape
    n = w.shape[1]
    tn = n // 4 if n % (4 * LANES) == 0 else n
    return pl.pallas_call(
        _mod_kernel,
        grid=(n // tn,),
        in_specs=[pl.BlockSpec((rows, d), lambda j: (0, 0)),
                  pl.BlockSpec((d, tn), lambda j: (0, j)),
                  pl.BlockSpec((1, tn), lambda j: (0, j))],
        out_specs=pl.BlockSpec((rows, tn), lambda j: (0, j)),
        out_shape=jax.ShapeDtypeStruct((rows, n), F32),
        compiler_params=_cparams(("arbitrary",)),
        name="adaln_mod",
    )(c_pad, w, b)


def _inproj_kernel(x_ref, mod_ref, w_ref, o_ref):
    u = x_ref[0] * (1.0 + mod_ref[0, 1:2, :]) + mod_ref[0, 0:1, :]
    o_ref[0] = _dot(u.astype(BF16), w_ref[...]).astype(BF16)


def _in_projection(x, mod, w, tm):
    b, s, d = x.shape
    n = w.shape[1]
    return pl.pallas_call(
        _inproj_kernel,
        grid=(b, s // tm),
        in_specs=[pl.BlockSpec((1, tm, d), lambda bi, i: (bi, i, 0)),
                  pl.BlockSpec((1, 6, d), lambda bi, i: (bi, 0, 0)),
                  pl.BlockSpec((d, n), lambda bi, i: (0, 0), pipeline_mode=pl.Buffered(1))],
        out_specs=pl.BlockSpec((1, tm, n), lambda bi, i: (bi, i, 0)),
        out_shape=jax.ShapeDtypeStruct((b, s, n), BF16),
        compiler_params=_cparams(("arbitrary", "arbitrary")),
        name="in_proj",
    )(x, mod, w)


def _store_values_transposed(v_ref, vt_ref):
    t = ATT_BLOCK
    for n in range(vt_ref.shape[0]):
        vt_ref[n] = v_ref[0, n * t:(n + 1) * t, :].astype(F32).T.astype(BF16)


def _head_rms(o0_t, o1_t, gain):
    def norm(o_t):
        ms = jnp.mean(o_t * o_t, axis=0, keepdims=True)
        return o_t * lax.rsqrt(ms + RMS_EPS)
    both = jnp.concatenate([norm(o0_t), norm(o1_t)], axis=0)
    return both.T * gain


def _masked_heads(q2):
    q2 = q2.astype(F32)
    lo = lax.broadcasted_iota(jnp.int32, q2.shape, 1) < HEAD_DIM
    return [jnp.where(lo, q2, 0.0).astype(BF16), jnp.where(lo, 0.0, q2).astype(BF16)]


def _sb_kernel(q_ref, k_ref, v_ref, g_ref, o_ref, vt_ref, acc_ref, z_ref, w_ref):
    t = ATT_BLOCK
    gi = pl.program_id(2)

    @pl.when(gi == 0)
    def _():
        _store_values_transposed(v_ref, vt_ref)

    qh = []
    for j in range(2):
        qh += _masked_heads(q_ref[0, j * t:(j + 1) * t, :])
    row = lax.broadcasted_iota(jnp.int32, (t, t), 0)
    col = lax.broadcasted_iota(jnp.int32, (t, t), 1)
    past = row < col
    tri = jnp.where(col >= row, 1.0, 0.0).astype(BF16)
    last_row = lax.broadcasted_iota(jnp.int32, (8, t), 0) == 7

    def key_block(r, j):
        return jnp.maximum(2 * gi + j - r, 0)

    def logits(r):
        for j in range(2):
            k2 = k_ref[0, pl.ds(pl.multiple_of(key_block(r, j) * t, t), t), :]
            for h in range(2):
                z_ref[r % 2, 2 * j + h] = _dot_nt(k2, qh[2 * j + h])

    def weights(r, carries, diag):
        out = []
        for s in range(4):
            carry = carries[s] if diag else jnp.where(2 * gi + s // 2 - r >= 0, carries[s], SB_PARKED)
            z = z_ref[r % 2, s]
            a = jnp.maximum(z, 0.0) + jnp.log(1.0 + jnp.exp2(jnp.abs(z) * -LOG2_E))
            if diag:
                a = jnp.where(past, a, 0.0)
            a = jnp.concatenate([a[:t - 8], a[t - 8:] + jnp.where(last_row, carry, 0.0)], axis=0)
            ainc = _dot(tri, a.astype(BF16))
            w = jnp.exp(z - ainc)
            if diag:
                w = jnp.where(past, w, 0.0)
            w_ref[r % 2, s] = w.astype(BF16)
            out.append(ainc[0:1, :])
        return out

    def values(r):
        for s in range(4):
            h = s % 2
            vt = vt_ref[key_block(r, s // 2), h * HEAD_DIM:(h + 1) * HEAD_DIM, :]
            acc_ref[s] += _dot(vt, w_ref[r % 2, s])

    def any_alive(carries, r_next):
        least = jnp.min(jnp.minimum(jnp.minimum(carries[0], carries[1]), jnp.minimum(carries[2], carries[3])))
        return jnp.logical_and(r_next <= 2 * gi + 1, least < SB_DEAD_LOG).astype(jnp.int32)

    acc_ref[...] = jnp.zeros(acc_ref.shape, F32)
    zero = jnp.zeros((1, t), F32)
    logits(0)
    logits(1)
    carries = weights(0, [zero] * 4, True)

    def body(st):
        r = st[1]
        values(r - 1)
        out = weights(r, st[2:], False)
        logits(r + 1)
        return (any_alive(out, r + 1), r + 1, *out)

    st = lax.while_loop(lambda st: st[0] > 0, body, (any_alive(carries, 1), jnp.int32(1), *carries))
    values(st[1] - 1)
    for j in range(2):
        o_ref[0, j * t:(j + 1) * t, :] = _head_rms(acc_ref[2 * j], acc_ref[2 * j + 1], g_ref[...]).astype(o_ref.dtype)


def _sb_attention(qkv, gain, n_pairs, q_blk0, k_blk0, v_blk0):
    b, s, _ = qkv.shape
    t = ATT_BLOCK
    return pl.pallas_call(
        _sb_kernel,
        grid=(b, n_pairs, s // (2 * t)),
        in_specs=[pl.BlockSpec((1, 2 * t, LANES), lambda bi, p, i: (bi, i, q_blk0 + p)),
                  pl.BlockSpec((1, s, LANES), lambda bi, p, i: (bi, 0, k_blk0 + p)),
                  pl.BlockSpec((1, s, LANES), lambda bi, p, i: (bi, 0, v_blk0 + p)),
                  pl.BlockSpec((1, LANES), lambda bi, p, i: (0, p))],
        out_specs=pl.BlockSpec((1, 2 * t, LANES), lambda bi, p, i: (bi, i, p)),
        out_shape=jax.ShapeDtypeStruct((b, s, n_pairs * LANES), BF16),
        scratch_shapes=[pltpu.VMEM((s // t, LANES, t), BF16),
                        pltpu.VMEM((4, HEAD_DIM, t), F32),
                        pltpu.VMEM((2, 4, t, t), F32),
                        pltpu.VMEM((2, 4, t, t), BF16)],
        compiler_params=_cparams(("arbitrary", "arbitrary", "arbitrary")),
        name="sb_attention",
    )(qkv, qkv, qkv, gain)


def _moba_gate_kernel(q_ref, k_ref, sel_ref):
    t = ATT_BLOCK
    s_len = k_ref.shape[1]
    nb = s_len // t
    q2 = q_ref[0]
    km = jnp.mean(k_ref[0].astype(F32).reshape(nb, t, LANES), axis=1)
    lane = lax.broadcasted_iota(jnp.int32, (nb, LANES), 1)
    blk = lax.broadcasted_iota(jnp.int32, (nb, s_len), 0)
    blk_f = blk.astype(F32)
    q_blk = lax.broadcasted_iota(jnp.int32, (nb, s_len), 1) // t
    fully_past = blk < q_blk
    for h in range(2):
        in_head = (lane < HEAD_DIM) if h == 0 else (lane >= HEAD_DIM)
        km_hi, km_lo = _split(jnp.where(in_head, km, 0.0))
        hi_lo = _dot_nt(jnp.concatenate([km_hi, km_lo], axis=0), q2)
        gate = hi_lo[:nb] + hi_lo[nb:]
        g = jnp.where(fully_past, gate, -jnp.inf)
        mask = jnp.full((nb, s_len), NEG_BIG, F32)
        for _ in range(MOBA_TOPK):
            mx = jnp.max(g, axis=0, keepdims=True)
            first = jnp.min(jnp.where(g == mx, blk_f, float(nb)), axis=0, keepdims=True)
            pick = blk_f == first
            mask = jnp.where(pick, 0.0, mask)
            g = jnp.where(pick, -jnp.inf, g)
        mask = jnp.where(fully_past, mask, NEG_BIG)
        sel_ref[0, 0, h] = jnp.where(blk == q_blk, 0.0, mask)


def _moba_gate(qkv, n_pairs, q_blk0, k_blk0):
    b, s, _ = qkv.shape
    nb = s // ATT_BLOCK
    return pl.pallas_call(
        _moba_gate_kernel,
        grid=(b, n_pairs),
        in_specs=[pl.BlockSpec((1, s, LANES), lambda bi, p: (bi, 0, q_blk0 + p)),
                  pl.BlockSpec((1, s, LANES), lambda bi, p: (bi, 0, k_blk0 + p))],
        out_specs=pl.BlockSpec((1, 1, 2, nb, s), lambda bi, p: (bi, p, 0, 0, 0)),
        out_shape=jax.ShapeDtypeStruct((b, n_pairs, 2, nb, s), F32),
        compiler_params=_cparams(("arbitrary", "arbitrary")),
        name="moba_gate",
    )(qkv, qkv)


def _moba_kernel(q_ref, k_ref, v_ref, sel_ref, bias_ref, g_ref, o_ref, vt_ref, s_ref, acc_ref):
    t = ATT_BLOCK
    gi = pl.program_id(2)

    @pl.when(gi == 0)
    def _():
        ones_row = jnp.where(lax.broadcasted_iota(jnp.int32, (MOBA_PAD_ROWS, t), 0) == 0, 1.0, 0.0).astype(BF16)
        for n in range(vt_ref.shape[0]):
            vt = v_ref[0, n * t:(n + 1) * t, :].astype(F32).T.astype(BF16)
            for h in range(2):
                vt_ref[n, h, :HEAD_DIM] = vt[h * HEAD_DIM:(h + 1) * HEAD_DIM]
                vt_ref[n, h, HEAD_DIM:] = ones_row

    qs = []
    for j in range(2):
        qs += _masked_heads(q_ref[0, j * t:(j + 1) * t, :])

    def tile_kind(n, j):
        return jnp.clip(n - (2 * gi + j) + 2, 0, 3)

    def score_blocks(first, count, mx):
        mx = list(mx)
        start = first * t if isinstance(first, int) else pl.multiple_of(first * t, t)
        keys = k_ref[0, pl.ds(start, count * t), :]
        for slot in range(4):
            j, h = slot // 2, slot % 2
            qk = _dot_nt(keys, qs[slot])
            for c in range(count):
                n = first + c
                s = qk[c * t:(c + 1) * t] + bias_ref[h, tile_kind(n, j)]
                s = s + sel_ref[0, 0, h, pl.ds(n, 1), j * t:(j + 1) * t]
                s_ref[slot, n] = s
                mx[slot] = jnp.maximum(mx[slot], jnp.max(s, axis=0, keepdims=True))
        return tuple(mx)

    def weigh_blocks(first, count, mx):
        blocks = [first + c for c in range(count)]
        for slot in range(4):
            p = [jnp.exp2((s_ref[slot, n] - mx[slot]).astype(BF16)) for n in blocks]
            vt = [vt_ref[n, slot % 2] for n in blocks]
            acc_ref[slot] += _dot(jnp.concatenate(vt, axis=1), jnp.concatenate(p, axis=0))

    n_blocks = 2 * gi + 2
    lead = n_blocks % 4
    mx = (jnp.full((1, t), -jnp.inf, F32),) * 4
    mx = lax.cond(lead > 0, lambda m: score_blocks(0, 2, m), lambda m: m, mx)
    mx = lax.fori_loop(0, n_blocks // 4, lambda i, m: score_blocks(lead + 4 * i, 4, m), mx)

    acc_ref[...] = jnp.zeros(acc_ref.shape, F32)

    @pl.when(lead > 0)
    def _():
        weigh_blocks(0, 2, mx)

    def weigh_trip(i, c):
        weigh_blocks(lead + 4 * i, 4, mx)
        return c

    lax.fori_loop(0, n_blocks // 4, weigh_trip, 0)
    for j in range(2):
        o = [acc_ref[2 * j + h, :HEAD_DIM] / acc_ref[2 * j + h, HEAD_DIM:HEAD_DIM + 1] for h in range(2)]
        o_ref[0, j * t:(j + 1) * t, :] = _head_rms(o[0], o[1], g_ref[...]).astype(o_ref.dtype)


def _moba_attention(qkv, sel, bias, gain, n_pairs, q_blk0, k_blk0, v_blk0):
    b, s, _ = qkv.shape
    t = ATT_BLOCK
    nb = s // t
    return pl.pallas_call(
        _moba_kernel,
        grid=(b, n_pairs, nb // 2),
        in_specs=[pl.BlockSpec((1, 2 * t, LANES), lambda bi, p, i: (bi, i, q_blk0 + p)),
                  pl.BlockSpec((1, s, LANES), lambda bi, p, i: (bi, 0, k_blk0 + p)),
                  pl.BlockSpec((1, s, LANES), lambda bi, p, i: (bi, 0, v_blk0 + p)),
                  pl.BlockSpec((1, 1, 2, nb, 2 * t), lambda bi, p, i: (bi, p, 0, 0, i)),
                  pl.BlockSpec((2, 4, t, t), lambda bi, p, i: (p, 0, 0, 0)),
                  pl.BlockSpec((1, LANES), lambda bi, p, i: (0, p))],
        out_specs=pl.BlockSpec((1, 2 * t, LANES), lambda bi, p, i: (bi, i, p)),
        out_shape=jax.ShapeDtypeStruct((b, s, n_pairs * LANES), BF16),
        scratch_shapes=[pltpu.VMEM((nb, 2, HEAD_DIM + MOBA_PAD_ROWS, t), BF16),
                        pltpu.VMEM((4, nb, t, t), F32),
                        pltpu.VMEM((4, HEAD_DIM + MOBA_PAD_ROWS, t), F32)],
        compiler_params=_cparams(("arbitrary", "arbitrary", "arbitrary")),
        name="moba_attention",
    )(qkv, qkv, qkv, sel, bias, gain)


def _mlp_kernel(osb_ref, omb_ref, x_ref, mod_ref, w1_ref, w2_ref, g1_ref, b1_ref,
                wg_ref, wu_ref, wd_ref, g2_ref, b2_ref, o_ref, *, alpha, chunks):
    tm = x_ref.shape[1]
    rows = [slice(0, tm // 2), slice(tm // 2, tm)]
    y = [_dot(osb_ref[0, r, :], w1_ref[...]) + _dot(omb_ref[0, r, :], w2_ref[...]) for r in rows]

    def norm1(k):
        x1 = _layer_norm(alpha * x_ref[0, rows[k], :] + mod_ref[0, 2:3, :] * y[k], g1_ref[...], b1_ref[...])
        return x1, (x1 * (1.0 + mod_ref[0, 4:5, :]) + mod_ref[0, 3:4, :]).astype(BF16)

    def feed_forward(u):
        f = None
        for c0, c1 in chunks:
            hg = _dot(u, wg_ref[:, c0:c1])
            hu = _dot(u, wu_ref[:, c0:c1])
            hidden = (hg * jax.nn.sigmoid(hg) * hu).astype(BF16)
            part = _dot(hidden, wd_ref[c0:c1, :])
            f = part if f is None else f + part
        return f

    def norm2(k, x1, f):
        o_ref[0, rows[k], :] = _layer_norm(alpha * x1 + mod_ref[0, 5:6, :] * f, g2_ref[...], b2_ref[...])

    xa, ua = norm1(0)
    fa = feed_forward(ua)
    xb, ub = norm1(1)
    fb = feed_forward(ub)
    norm2(0, xa, fa)
    norm2(1, xb, fb)


def _hidden_chunks(d_ff):
    edges = list(range(0, d_ff, FF_CHUNK)) + [d_ff]
    return tuple(zip(edges[:-1], edges[1:]))


def _mlp(o_sb, o_mb, x, mod, w1, w2, ln1_g, ln1_b, wg, wu, wd, ln2_g, ln2_b, alpha, tm):
    b, s, d = x.shape
    h1, h2 = w1.shape[0], w2.shape[0]
    d_ff = wg.shape[1]
    const = lambda bi, i: (0, 0)
    resident = lambda shape: pl.BlockSpec(shape, const, pipeline_mode=pl.Buffered(1))
    return pl.pallas_call(
        functools.partial(_mlp_kernel, alpha=alpha, chunks=_hidden_chunks(d_ff)),
        grid=(b, s // tm),
        in_specs=[pl.BlockSpec((1, tm, h1), lambda bi, i: (bi, i, 0)),
                  pl.BlockSpec((1, tm, h2), lambda bi, i: (bi, i, 0)),
                  pl.BlockSpec((1, tm, d), lambda bi, i: (bi, i, 0)),
                  pl.BlockSpec((1, 6, d), lambda bi, i: (bi, 0, 0)),
                  resident((h1, d)), resident((h2, d)), resident((1, d)), resident((1, d)),
                  resident((d, d_ff)), resident((d, d_ff)), resident((d_ff, d)),
                  resident((1, d)), resident((1, d))],
        out_specs=pl.BlockSpec((1, tm, d), lambda bi, i: (bi, i, 0)),
        out_shape=jax.ShapeDtypeStruct((b, s, d), F32),
        compiler_params=_cparams(("arbitrary", "arbitrary")),
        name="out_proj_ffn",
    )(o_sb, o_mb, x, mod, w1, w2, ln1_g, ln1_b, wg, wu, wd, ln2_g, ln2_b)


def _t5_bucket(dist):
    n = np.maximum(dist, 0)
    max_exact = N_BUCKETS // 2
    nf = np.maximum(n, 1).astype(np.float32)
    large = max_exact + (np.log(nf / np.float32(max_exact)) / np.float32(math.log(MAX_DISTANCE / max_exact))
                         * np.float32(N_BUCKETS - max_exact)).astype(np.int32)
    large = np.minimum(large, N_BUCKETS - 1)
    return np.where(n < max_exact, n, large)


def _bucket_lookup(rel_h, bucket):
    out = jnp.zeros((rel_h.shape[0],) + bucket.shape, F32)
    for b in range(N_BUCKETS):
        out = jnp.where((bucket == b)[None], rel_h[:, b].reshape((-1,) + (1,) * bucket.ndim), out)
    return out


def _moba_bias_tiles(rel_bias):
    t = ATT_BLOCK
    j = np.arange(t)[:, None]
    i = np.arange(t)[None, :]
    rel_h = rel_bias.T.astype(F32) * LOG2_E
    n_heads = rel_h.shape[0]
    by_dist = _bucket_lookup(rel_h, _t5_bucket(np.arange(2 * t)))

    def toeplitz(v):
        return jnp.tile(v, (1, t))[:, :t * (2 * t - 1)].reshape(n_heads, t, 2 * t - 1)[:, :, :t]

    far = jnp.broadcast_to(by_dist[:, 2 * t - 1][:, None, None], (n_heads, t, t))
    prev = toeplitz(jnp.roll(by_dist, -t, axis=1))
    own = jnp.where((j <= i)[None], toeplitz(by_dist), NEG_BIG)
    return jnp.stack([far, prev, own, jnp.full_like(far, NEG_BIG)], axis=1)


def kernel(x, c, w_ada, b_ada, w_in, sb_gain, mb_gain, w_out, ln1_g, ln1_b,
           w_gate, w_up, w_down, ln2_g, ln2_b, rel_bias):
    depth = w_ada.shape[0]
    b, s, d = x.shape
    d_sb = sb_gain.shape[1]
    d_mb = mb_gain.shape[1]
    assert ATT_BLOCK == MOBA_BLOCK and MOBA_BLOCK >= 2 * MAX_DISTANCE
    assert s % ATT_BLOCK == 0 and (s // ATT_BLOCK) % 8 == 0
    assert d_sb % LANES == 0 and d_mb % LANES == 0 and d % LANES == 0
    alpha = float((2 * depth) ** 0.25)
    tm = min(512, s)
    sb_pairs, mb_pairs = d_sb // LANES, d_mb // LANES

    c_pad = jnp.pad(c, ((0, (-c.shape[0]) % 8), (0, 0)))
    bias_tiles = _moba_bias_tiles(rel_bias)
    col_scale = jnp.concatenate([jnp.full((d_sb,), HEAD_DIM ** -0.5, F32), jnp.ones((2 * d_sb,), F32),
                                 jnp.full((d_mb,), HEAD_DIM ** -0.5 * LOG2_E, F32), jnp.ones((2 * d_mb,), F32)])
    mb0 = 3 * sb_pairs

    for l in range(depth):
        mod = _modulation(c_pad, w_ada[l], b_ada[l][None, :])[:b].reshape(b, 6, d)
        qkv = _in_projection(x, mod, (w_in[l] * col_scale[None, :]).astype(BF16), min(2 * tm, s))

        o_sb = _sb_attention(qkv, sb_gain[l][None, :], sb_pairs, 0, sb_pairs, 2 * sb_pairs)
        sel = _moba_gate(qkv, mb_pairs, mb0, mb0 + mb_pairs)
        o_mb = _moba_attention(qkv, sel, bias_tiles, mb_gain[l][None, :], mb_pairs,
                               mb0, mb0 + mb_pairs, mb0 + 2 * mb_pairs)

        wo = w_out[l].astype(BF16)
        x = _mlp(o_sb, o_mb, x, mod, wo[:d_sb], wo[d_sb:], ln1_g[l][None, :], ln1_b[l][None, :],
                 w_gate[l].astype(BF16), w_up[l].astype(BF16), w_down[l].astype(BF16),
                 ln2_g[l][None, :], ln2_b[l][None, :], alpha, tm)
    return x
```

```python
import functools
import math

import jax
import jax.numpy as jnp
import numpy as np
from jax import lax
from jax.experimental import pallas as pl
from jax.experimental.pallas import tpu as pltpu

HEAD_DIM = 64
MOBA_BLOCK = 256
MOBA_TOPK = 3
N_BUCKETS = 32
MAX_DISTANCE = 128
LN_EPS = 1e-5
RMS_EPS = 1e-6
NEG_BIG = -1e30
SB_DEAD_LOG = 120.0
LOG2_E = 1.4426950408889634
FF_CHUNK = 1024
MOBA_PAD_ROWS = 16
SB_PARKED = 1e30
LANES = 128
ATT_BLOCK = 256
VMEM_LIMIT = 56 * 1024 * 1024

F32 = jnp.float32
BF16 = jnp.bfloat16


def _cparams(sem):
    return pltpu.CompilerParams(dimension_semantics=sem, vmem_limit_bytes=VMEM_LIMIT)


def _split(a):
    hi = a.astype(BF16)
    lo = (a - hi.astype(F32)).astype(BF16)
    return hi, lo


def _dot(a, b):
    return jnp.dot(a, b, preferred_element_type=F32)


def _dot_nt(a, b):
    return lax.dot_general(a, b, (((1,), (1,)), ((), ())), preferred_element_type=F32)


def _layer_norm(r, g, b):
    mu = jnp.mean(r, axis=-1, keepdims=True)
    d = r - mu
    var = jnp.mean(d * d, axis=-1, keepdims=True)
    return d * lax.rsqrt(var + LN_EPS) * g + b


def _mod_kernel(c_ref, w_ref, b_ref, o_ref):
    c = c_ref[...]
    cond = c * jax.nn.sigmoid(c)
    c_hi, c_lo = _split(cond)
    w_hi, w_lo = _split(w_ref[...])
    o_ref[...] = _dot(c_hi, w_hi) + _dot(c_hi, w_lo) + _dot(c_lo, w_hi) + b_ref[...]


def _modulation(c_pad, w, b):
    rows, d = c_pad.shape
    n = w.shape[1]
    tn = n // 4 if n % (4 * LANES) == 0 else n
    return pl.pallas_call(
        _mod_kernel,
        grid=(n // tn,),
        in_specs=[pl.BlockSpec((rows, d), lambda j: (0, 0)),
                  pl.BlockSpec((d, tn), lambda j: (0, j)),
                  pl.BlockSpec((1, tn), lambda j: (0, j))],
        out_specs=pl.BlockSpec((rows, tn), lambda j: (0, j)),
        out_shape=jax.ShapeDtypeStruct((rows, n), F32),
        compiler_params=_cparams(("arbitrary",)),
        name="adaln_mod",
    )(c_pad, w, b)


def _inproj_kernel(x_ref, mod_ref, w_ref, o_ref):
    u = x_ref[0] * (1.0 + mod_ref[0, 1:2, :]) + mod_ref[0, 0:1, :]
    o_ref[0] = _dot(u.astype(BF16), w_ref[...]).astype(BF16)


def _in_projection(x, mod, w, tm):
    b, s, d = x.shape
    n = w.shape[1]
    return pl.pallas_call(
        _inproj_kernel,
        grid=(b, s // tm),
        in_specs=[pl.BlockSpec((1, tm, d), lambda bi, i: (bi, i, 0)),
                  pl.BlockSpec((1, 6, d), lambda bi, i: (bi, 0, 0)),
                  pl.BlockSpec((d, n), lambda bi, i: (0, 0), pipeline_mode=pl.Buffered(1))],
        out_specs=pl.BlockSpec((1, tm, n), lambda bi, i: (bi, i, 0)),
        out_shape=jax.ShapeDtypeStruct((b, s, n), BF16),
        compiler_params=_cparams(("arbitrary", "arbitrary")),
        name="in_proj",
    )(x, mod, w)


def _store_values_transposed(v_ref, vt_ref):
    t = ATT_BLOCK
    for n in range(vt_ref.shape[0]):
        vt_ref[n] = v_ref[0, n * t:(n + 1) * t, :].astype(F32).T.astype(BF16)


def _head_rms(o0_t, o1_t, gain):
    def norm(o_t):
        ms = jnp.mean(o_t * o_t, axis=0, keepdims=True)
        return o_t * lax.rsqrt(ms + RMS_EPS)
    both = jnp.concatenate([norm(o0_t), norm(o1_t)], axis=0)
    return both.T * gain


def _masked_heads(q2):
    q2 = q2.astype(F32)
    lo = lax.broadcasted_iota(jnp.int32, q2.shape, 1) < HEAD_DIM
    return [jnp.where(lo, q2, 0.0).astype(BF16), jnp.where(lo, 0.0, q2).astype(BF16)]


def _sb_kernel(q_ref, k_ref, v_ref, g_ref, o_ref, vt_ref, acc_ref, z_ref, w_ref):
    t = ATT_BLOCK
    gi = pl.program_id(2)

    @pl.when(gi == 0)
    def _():
        _store_values_transposed(v_ref, vt_ref)

    qh = []
    for j in range(2):
        qh += _masked_heads(q_ref[0, j * t:(j + 1) * t, :])
    row = lax.broadcasted_iota(jnp.int32, (t, t), 0)
    col = lax.broadcasted_iota(jnp.int32, (t, t), 1)
    past = row < col
    tri = jnp.where(col >= row, 1.0, 0.0).astype(BF16)
    last_row = lax.broadcasted_iota(jnp.int32, (8, t), 0) == 7

    def key_block(r, j):
        return jnp.maximum(2 * gi + j - r, 0)

    def logits(r):
        for j in range(2):
            k2 = k_ref[0, pl.ds(pl.multiple_of(key_block(r, j) * t, t), t), :]
            for h in range(2):
                z_ref[r % 2, 2 * j + h] = _dot_nt(k2, qh[2 * j + h])

    def weights(r, carries, diag):
        out = []
        for s in range(4):
            carry = carries[s] if diag else jnp.where(2 * gi + s // 2 - r >= 0, carries[s], SB_PARKED)
            z = z_ref[r % 2, s]
            a = jnp.maximum(z, 0.0) + jnp.log(1.0 + jnp.exp2(jnp.abs(z) * -LOG2_E))
            if diag:
                a = jnp.where(past, a, 0.0)
            a = jnp.concatenate([a[:t - 8], a[t - 8:] + jnp.where(last_row, carry, 0.0)], axis=0)
            ainc = _dot(tri, a.astype(BF16))
            w = jnp.exp(z - ainc)
            if diag:
                w = jnp.where(past, w, 0.0)
            w_ref[r % 2, s] = w.astype(BF16)
            out.append(ainc[0:1, :])
        return out

    def values(r):
        for s in range(4):
            h = s % 2
            vt = vt_ref[key_block(r, s // 2), h * HEAD_DIM:(h + 1) * HEAD_DIM, :]
            acc_ref[s] += _dot(vt, w_ref[r % 2, s])

    def any_alive(carries, r_next):
        least = jnp.min(jnp.minimum(jnp.minimum(carries[0], carries[1]), jnp.minimum(carries[2], carries[3])))
        return jnp.logical_and(r_next <= 2 * gi + 1, least < SB_DEAD_LOG).astype(jnp.int32)

    acc_ref[...] = jnp.zeros(acc_ref.shape, F32)
    zero = jnp.zeros((1, t), F32)
    logits(0)
    logits(1)
    carries = weights(0, [zero] * 4, True)
    values(0)
    carries = weights(1, carries, False)
    logits(2)

    def body(st):
        r = st[1]
        values(r - 1)
        out = weights(r, st[2:], False)
        logits(r + 1)
        return (any_alive(out, r + 1), r + 1, *out)

    st = lax.while_loop(lambda st: st[0] > 0, body, (any_alive(carries, 2), jnp.int32(2), *carries))
    values(st[1] - 1)
    for j in range(2):
        o_ref[0, j * t:(j + 1) * t, :] = _head_rms(acc_ref[2 * j], acc_ref[2 * j + 1], g_ref[...]).astype(o_ref.dtype)


def _sb_attention(qkv, gain, n_pairs, q_blk0, k_blk0, v_blk0):
    b, s, _ = qkv.shape
    t = ATT_BLOCK
    return pl.pallas_call(
        _sb_kernel,
        grid=(b, n_pairs, s // (2 * t)),
        in_specs=[pl.BlockSpec((1, 2 * t, LANES), lambda bi, p, i: (bi, i, q_blk0 + p)),
                  pl.BlockSpec((1, s, LANES), lambda bi, p, i: (bi, 0, k_blk0 + p)),
                  pl.BlockSpec((1, s, LANES), lambda bi, p, i: (bi, 0, v_blk0 + p)),
                  pl.BlockSpec((1, LANES), lambda bi, p, i: (0, p))],
        out_specs=pl.BlockSpec((1, 2 * t, LANES), lambda bi, p, i: (bi, i, p)),
        out_shape=jax.ShapeDtypeStruct((b, s, n_pairs * LANES), BF16),
        scratch_shapes=[pltpu.VMEM((s // t, LANES, t), BF16),
                        pltpu.VMEM((4, HEAD_DIM, t), F32),
                        pltpu.VMEM((2, 4, t, t), F32),
                        pltpu.VMEM((2, 4, t, t), BF16)],
        compiler_params=_cparams(("arbitrary", "arbitrary", "arbitrary")),
        name="sb_attention",
    )(qkv, qkv, qkv, gain)


def _moba_gate_kernel(q_ref, k_ref, sel_ref):
    t = ATT_BLOCK
    s_len = k_ref.shape[1]
    nb = s_len // t
    q2 = q_ref[0]
    km = jnp.mean(k_ref[0].astype(F32).reshape(nb, t, LANES), axis=1)
    lane = lax.broadcasted_iota(jnp.int32, (nb, LANES), 1)
    blk = lax.broadcasted_iota(jnp.int32, (nb, s_len), 0)
    blk_f = blk.astype(F32)
    q_blk = lax.broadcasted_iota(jnp.int32, (nb, s_len), 1) // t
    fully_past = blk < q_blk
    for h in range(2):
        in_head = (lane < HEAD_DIM) if h == 0 else (lane >= HEAD_DIM)
        km_hi, km_lo = _split(jnp.where(in_head, km, 0.0))
        hi_lo = _dot_nt(jnp.concatenate([km_hi, km_lo], axis=0), q2)
        gate = hi_lo[:nb] + hi_lo[nb:]
        g = jnp.where(fully_past, gate, -jnp.inf)
        mask = jnp.full((nb, s_len), NEG_BIG, F32)
        for _ in range(MOBA_TOPK):
            mx = jnp.max(g, axis=0, keepdims=True)
            first = jnp.min(jnp.where(g == mx, blk_f, float(nb)), axis=0, keepdims=True)
            pick = blk_f == first
            mask = jnp.where(pick, 0.0, mask)
            g = jnp.where(pick, -jnp.inf, g)
        mask = jnp.where(fully_past, mask, NEG_BIG)
        sel_ref[0, 0, h] = jnp.where(blk == q_blk, 0.0, mask)


def _moba_gate(qkv, n_pairs, q_blk0, k_blk0):
    b, s, _ = qkv.shape
    nb = s // ATT_BLOCK
    return pl.pallas_call(
        _moba_gate_kernel,
        grid=(b, n_pairs),
        in_specs=[pl.BlockSpec((1, s, LANES), lambda bi, p: (bi, 0, q_blk0 + p)),
                  pl.BlockSpec((1, s, LANES), lambda bi, p: (bi, 0, k_blk0 + p))],
        out_specs=pl.BlockSpec((1, 1, 2, nb, s), lambda bi, p: (bi, p, 0, 0, 0)),
        out_shape=jax.ShapeDtypeStruct((b, n_pairs, 2, nb, s), F32),
        compiler_params=_cparams(("arbitrary", "arbitrary")),
        name="moba_gate",
    )(qkv, qkv)


def _moba_kernel(q_ref, k_ref, v_ref, sel_ref, bias_ref, g_ref, o_ref, vt_ref, s_ref, acc_ref):
    t = ATT_BLOCK
    gi = pl.program_id(2)

    @pl.when(gi == 0)
    def _():
        ones_row = jnp.where(lax.broadcasted_iota(jnp.int32, (MOBA_PAD_ROWS, t), 0) == 0, 1.0, 0.0).astype(BF16)
        for n in range(vt_ref.shape[0]):
            vt = v_ref[0, n * t:(n + 1) * t, :].astype(F32).T.astype(BF16)
            for h in range(2):
                vt_ref[n, h, :HEAD_DIM] = vt[h * HEAD_DIM:(h + 1) * HEAD_DIM]
                vt_ref[n, h, HEAD_DIM:] = ones_row

    qs = []
    for j in range(2):
        qs += _masked_heads(q_ref[0, j * t:(j + 1) * t, :])

    def tile_kind(n, j):
        return jnp.clip(n - (2 * gi + j) + 2, 0, 3)

    def score_blocks(first, count, mx):
        mx = list(mx)
        start = first * t if isinstance(first, int) else pl.multiple_of(first * t, t)
        keys = k_ref[0, pl.ds(start, count * t), :]
        for slot in range(4):
            j, h = slot // 2, slot % 2
            qk = _dot_nt(keys, qs[slot])
            for c in range(count):
                n = first + c
                s = qk[c * t:(c + 1) * t] + bias_ref[h, tile_kind(n, j)]
                s = s + sel_ref[0, 0, h, pl.ds(n, 1), j * t:(j + 1) * t]
                s_ref[slot, n] = s
                mx[slot] = jnp.maximum(mx[slot], jnp.max(s, axis=0, keepdims=True))
        return tuple(mx)

    def weigh_blocks(first, count, mx):
        blocks = [first + c for c in range(count)]
        for slot in range(4):
            p = [jnp.exp2((s_ref[slot, n] - mx[slot]).astype(BF16)) for n in blocks]
            vt = [vt_ref[n, slot % 2] for n in blocks]
            acc_ref[slot] += _dot(jnp.concatenate(vt, axis=1), jnp.concatenate(p, axis=0))

    n_blocks = 2 * gi + 2
    lead = n_blocks % 4
    mx = (jnp.full((1, t), -jnp.inf, F32),) * 4
    mx = lax.cond(lead > 0, lambda m: score_blocks(0, 2, m), lambda m: m, mx)
    mx = lax.fori_loop(0, n_blocks // 4, lambda i, m: score_blocks(lead + 4 * i, 4, m), mx)

    acc_ref[...] = jnp.zeros(acc_ref.shape, F32)

    @pl.when(lead > 0)
    def _():
        weigh_blocks(0, 2, mx)

    def weigh_trip(i, c):
        weigh_blocks(lead + 4 * i, 4, mx)
        return c

    lax.fori_loop(0, n_blocks // 4, weigh_trip, 0)
    for j in range(2):
        o = [acc_ref[2 * j + h, :HEAD_DIM] / acc_ref[2 * j + h, HEAD_DIM:HEAD_DIM + 1] for h in range(2)]
        o_ref[0, j * t:(j + 1) * t, :] = _head_rms(o[0], o[1], g_ref[...]).astype(o_ref.dtype)


def _moba_attention(qkv, sel, bias, gain, n_pairs, q_blk0, k_blk0, v_blk0):
    b, s, _ = qkv.shape
    t = ATT_BLOCK
    nb = s // t
    return pl.pallas_call(
        _moba_kernel,
        grid=(b, n_pairs, nb // 2),
        in_specs=[pl.BlockSpec((1, 2 * t, LANES), lambda bi, p, i: (bi, i, q_blk0 + p)),
                  pl.BlockSpec((1, s, LANES), lambda bi, p, i: (bi, 0, k_blk0 + p)),
                  pl.BlockSpec((1, s, LANES), lambda bi, p, i: (bi, 0, v_blk0 + p)),
                  pl.BlockSpec((1, 1, 2, nb, 2 * t), lambda bi, p, i: (bi, p, 0, 0, i)),
                  pl.BlockSpec((2, 4, t, t), lambda bi, p, i: (p, 0, 0, 0)),
                  pl.BlockSpec((1, LANES), lambda bi, p, i: (0, p))],
        out_specs=pl.BlockSpec((1, 2 * t, LANES), lambda bi, p, i: (bi, i, p)),
        out_shape=jax.ShapeDtypeStruct((b, s, n_pairs * LANES), BF16),
        scratch_shapes=[pltpu.VMEM((nb, 2, HEAD_DIM + MOBA_PAD_ROWS, t), BF16),
                        pltpu.VMEM((4, nb, t, t), F32),
                        pltpu.VMEM((4, HEAD_DIM + MOBA_PAD_ROWS, t), F32)],
        compiler_params=_cparams(("arbitrary", "arbitrary", "arbitrary")),
        name="moba_attention",
    )(qkv, qkv, qkv, sel, bias, gain)


def _mlp_kernel(osb_ref, omb_ref, x_ref, mod_ref, w1_ref, w2_ref, g1_ref, b1_ref,
                wg_ref, wu_ref, wd_ref, g2_ref, b2_ref, o_ref, *, alpha, chunks):
    tm = x_ref.shape[1]
    rows = [slice(0, tm // 2), slice(tm // 2, tm)]
    y = [_dot(osb_ref[0, r, :], w1_ref[...]) + _dot(omb_ref[0, r, :], w2_ref[...]) for r in rows]

    def norm1(k):
        x1 = _layer_norm(alpha * x_ref[0, rows[k], :] + mod_ref[0, 2:3, :] * y[k], g1_ref[...], b1_ref[...])
        return x1, (x1 * (1.0 + mod_ref[0, 4:5, :]) + mod_ref[0, 3:4, :]).astype(BF16)

    def feed_forward(u):
        f = None
        for c0, c1 in chunks:
            hg = _dot(u, wg_ref[:, c0:c1])
            hu = _dot(u, wu_ref[:, c0:c1])
            hidden = (hg * jax.nn.sigmoid(hg) * hu).astype(BF16)
            part = _dot(hidden, wd_ref[c0:c1, :])
            f = part if f is None else f + part
        return f

    def norm2(k, x1, f):
        o_ref[0, rows[k], :] = _layer_norm(alpha * x1 + mod_ref[0, 5:6, :] * f, g2_ref[...], b2_ref[...])

    xa, ua = norm1(0)
    fa = feed_forward(ua)
    xb, ub = norm1(1)
    fb = feed_forward(ub)
    norm2(0, xa, fa)
    norm2(1, xb, fb)


def _hidden_chunks(d_ff):
    edges = list(range(0, d_ff, FF_CHUNK)) + [d_ff]
    return tuple(zip(edges[:-1], edges[1:]))


def _mlp(o_sb, o_mb, x, mod, w1, w2, ln1_g, ln1_b, wg, wu, wd, ln2_g, ln2_b, alpha, tm):
    b, s, d = x.shape
    h1, h2 = w1.shape[0], w2.shape[0]
    d_ff = wg.shape[1]
    const = lambda bi, i: (0, 0)
    resident = lambda shape: pl.BlockSpec(shape, const, pipeline_mode=pl.Buffered(1))
    return pl.pallas_call(
        functools.partial(_mlp_kernel, alpha=alpha, chunks=_hidden_chunks(d_ff)),
        grid=(b, s // tm),
        in_specs=[pl.BlockSpec((1, tm, h1), lambda bi, i: (bi, i, 0)),
                  pl.BlockSpec((1, tm, h2), lambda bi, i: (bi, i, 0)),
                  pl.BlockSpec((1, tm, d), lambda bi, i: (bi, i, 0)),
                  pl.BlockSpec((1, 6, d), lambda bi, i: (bi, 0, 0)),
                  resident((h1, d)), resident((h2, d)), resident((1, d)), resident((1, d)),
                  resident((d, d_ff)), resident((d, d_ff)), resident((d_ff, d)),
                  resident((1, d)), resident((1, d))],
        out_specs=pl.BlockSpec((1, tm, d), lambda bi, i: (bi, i, 0)),
        out_shape=jax.ShapeDtypeStruct((b, s, d), F32),
        compiler_params=_cparams(("arbitrary", "arbitrary")),
        name="out_proj_ffn",
    )(o_sb, o_mb, x, mod, w1, w2, ln1_g, ln1_b, wg, wu, wd, ln2_g, ln2_b)


def _t5_bucket(dist):
    n = np.maximum(dist, 0)
    max_exact = N_BUCKETS // 2
    nf = np.maximum(n, 1).astype(np.float32)
    large = max_exact + (np.log(nf / np.float32(max_exact)) / np.float32(math.log(MAX_DISTANCE / max_exact))
                         * np.float32(N_BUCKETS - max_exact)).astype(np.int32)
    large = np.minimum(large, N_BUCKETS - 1)
    return np.where(n < max_exact, n, large)


def _bucket_lookup(rel_h, bucket):
    out = jnp.zeros((rel_h.shape[0],) + bucket.shape, F32)
    for b in range(N_BUCKETS):
        out = jnp.where((bucket == b)[None], rel_h[:, b].reshape((-1,) + (1,) * bucket.ndim), out)
    return out


def _moba_bias_tiles(rel_bias):
    t = ATT_BLOCK
    j = np.arange(t)[:, None]
    i = np.arange(t)[None, :]
    rel_h = rel_bias.T.astype(F32) * LOG2_E
    n_heads = rel_h.shape[0]
    by_dist = _bucket_lookup(rel_h, _t5_bucket(np.arange(2 * t)))

    def toeplitz(v):
        return jnp.tile(v, (1, t))[:, :t * (2 * t - 1)].reshape(n_heads, t, 2 * t - 1)[:, :, :t]

    far = jnp.broadcast_to(by_dist[:, 2 * t - 1][:, None, None], (n_heads, t, t))
    prev = toeplitz(jnp.roll(by_dist, -t, axis=1))
    own = jnp.where((j <= i)[None], toeplitz(by_dist), NEG_BIG)
    return jnp.stack([far, prev, own, jnp.full_like(far, NEG_BIG)], axis=1)


def kernel(x, c, w_ada, b_ada, w_in, sb_gain, mb_gain, w_out, ln1_g, ln1_b,
           w_gate, w_up, w_down, ln2_g, ln2_b, rel_bias):
    depth = w_ada.shape[0]
    b, s, d = x.shape
    d_sb = sb_gain.shape[1]
    d_mb = mb_gain.shape[1]
    assert ATT_BLOCK == MOBA_BLOCK and MOBA_BLOCK >= 2 * MAX_DISTANCE
    assert s % ATT_BLOCK == 0 and (s // ATT_BLOCK) % 8 == 0
    assert d_sb % LANES == 0 and d_mb % LANES == 0 and d % LANES == 0
    alpha = float((2 * depth) ** 0.25)
    tm = min(512, s)
    sb_pairs, mb_pairs = d_sb // LANES, d_mb // LANES

    c_pad = jnp.pad(c, ((0, (-c.shape[0]) % 8), (0, 0)))
    bias_tiles = _moba_bias_tiles(rel_bias)
    col_scale = jnp.concatenate([jnp.full((d_sb,), HEAD_DIM ** -0.5, F32), jnp.ones((2 * d_sb,), F32),
                                 jnp.full((d_mb,), HEAD_DIM ** -0.5 * LOG2_E, F32), jnp.ones((2 * d_mb,), F32)])
    mb0 = 3 * sb_pairs

    for l in range(depth):
        mod = _modulation(c_pad, w_ada[l], b_ada[l][None, :])[:b].reshape(b, 6, d)
        qkv = _in_projection(x, mod, (w_in[l] * col_scale[None, :]).astype(BF16), min(2 * tm, s))

        o_sb = _sb_attention(qkv, sb_gain[l][None, :], sb_pairs, 0, sb_pairs, 2 * sb_pairs)
        sel = _moba_gate(qkv, mb_pairs, mb0, mb0 + mb_pairs)
        o_mb = _moba_attention(qkv, sel, bias_tiles, mb_gain[l][None, :], mb_pairs,
                               mb0, mb0 + mb_pairs, mb0 + 2 * mb_pairs)

        wo = w_out[l].astype(BF16)
        x = _mlp(o_sb, o_mb, x, mod, wo[:d_sb], wo[d_sb:], ln1_g[l][None, :], ln1_b[l][None, :],
                 w_gate[l].astype(BF16), w_up[l].astype(BF16), w_down[l].astype(BF16),
                 ln2_g[l][None, :], ln2_b[l][None, :], alpha, tm)
    return x
```
